```python
import math
import jax, jax.numpy as jnp
from jax import lax
import numpy as np

D_MODEL = 2048
BATCH = 4
SEQ = 4096
DEPTH = 2

N_MIXERS = 2
N_HEADS = 16
Q_LORA = 512
KV_LORA = 512
QK_NOPE = 128
QK_ROPE = 64
V_HEAD = 128
QK_HEAD = QK_NOPE + QK_ROPE
ROPE_THETA = 10000.0
Q_BLOCK = 128
POOL_WINDOWS = (2, 4, 8, 16)
N_POOL_GROUPS = len(POOL_WINDOWS)
POOL_GROUP = D_MODEL // N_POOL_GROUPS
D_FF = 4 * D_MODEL
EPS = 1e-6
N_ATTN_LAYERS = (DEPTH + 1) // 2
N_POOL_LAYERS = DEPTH // 2

kernel_name = "hybrid_mla_multiscale_pool_sqrelu"


def rmsnorm(x, g):
    x32 = x.astype(jnp.float32)
    y = x32 * lax.rsqrt(jnp.mean(x32 * x32, axis=-1, keepdims=True) + EPS)
    return (y * g.astype(jnp.float32)).astype(x.dtype)


def rope_tables(positions):
    inv_freq = 1.0 / (ROPE_THETA ** (jnp.arange(0, QK_ROPE, 2, dtype=jnp.float32) / QK_ROPE))
    ang = positions.astype(jnp.float32)[..., None] * inv_freq
    return jnp.cos(ang), jnp.sin(ang)


def apply_rope(x, cos, sin):
    x32 = x.astype(jnp.float32)
    x1, x2 = jnp.split(x32, 2, axis=-1)
    out = jnp.concatenate([x1 * cos - x2 * sin, x2 * cos + x1 * sin], axis=-1)
    return out.astype(x.dtype)


def latent_attention(x, positions, w_dqkv, q_norm, kv_norm, w_uq, w_ukv, w_o):
    B, S, _ = x.shape
    c = x @ w_dqkv
    c_q = rmsnorm(c[..., :Q_LORA], q_norm)
    c_kv = rmsnorm(c[..., Q_LORA:Q_LORA + KV_LORA], kv_norm)
    k_rope = c[..., Q_LORA + KV_LORA:][:, :, None, :]

    q = (c_q @ w_uq).reshape(B, S, N_HEADS, QK_HEAD)
    kv = (c_kv @ w_ukv).reshape(B, S, N_HEADS, QK_NOPE + V_HEAD)
    k_nope, v = kv[..., :QK_NOPE], kv[..., QK_NOPE:]

    cos, sin = rope_tables(positions)
    cos, sin = cos[:, :, None, :], sin[:, :, None, :]
    q = jnp.concatenate([q[..., :QK_NOPE], apply_rope(q[..., QK_NOPE:], cos, sin)], axis=-1)
    k_rope = apply_rope(k_rope, cos, sin)
    k = jnp.concatenate([k_nope, jnp.broadcast_to(k_rope, (B, S, N_HEADS, QK_ROPE))], axis=-1)

    scale = 1.0 / math.sqrt(QK_HEAD)
    n_blocks = S // Q_BLOCK
    q_blocks = q.reshape(B, n_blocks, Q_BLOCK, N_HEADS, QK_HEAD).transpose(1, 0, 2, 3, 4)
    key_pos = jnp.arange(S)

    def attend(args):
        qb, bi = args
        s = jnp.einsum('bqhd,bkhd->bhqk', qb, k).astype(jnp.float32) * scale
        q_pos = bi * Q_BLOCK + jnp.arange(Q_BLOCK)
        mask = key_pos[None, :] <= q_pos[:, None]
        s = jnp.where(mask[None, None], s, -jnp.inf)
        p = jax.nn.softmax(s, axis=-1).astype(v.dtype)
        return jnp.einsum('bhqk,bkhd->bqhd', p, v)

    o = lax.map(attend, (q_blocks, jnp.arange(n_blocks)))
    o = o.transpose(1, 0, 2, 3, 4).reshape(B, S, N_HEADS * V_HEAD)
    return o @ w_o


def multiscale_pool(x, w_pool, pool_scale):
    B, S, D = x.shape
    xg = x.astype(jnp.float32).reshape(B, S, N_POOL_GROUPS, POOL_GROUP)
    cs = jnp.cumsum(xg, axis=1)
    t = jnp.arange(S)
    pooled = []
    for g, w in enumerate(POOL_WINDOWS):
        csg = cs[:, :, g]
        prev = jnp.pad(csg, ((0, 0), (w, 0), (0, 0)))[:, :S]
        cnt = jnp.minimum(t + 1, w).astype(jnp.float32)[None, :, None]
        pooled.append((csg - prev) / cnt)
    y = (jnp.stack(pooled, axis=2) - xg).astype(x.dtype)
    y = jnp.einsum('bsgc,gcd->bsgd', y, w_pool).reshape(B, S, D)
    return y * pool_scale


def sqrelu_mlp(x, w_up, w_down):
    h = jax.nn.relu(x @ w_up)
    return (h * h) @ w_down


def setup_inputs(seed: int = 0) -> dict:
    key = jax.random.key(seed)
    ks = jax.random.split(key, 16)
    f32 = jnp.float32

    def nrm(k, shape, fan_in):
        return jax.random.normal(k, shape, f32) * (fan_in ** -0.5)

    def gain(k, shape):
        return 1.0 + 0.05 * jax.random.normal(k, shape, f32)

    na, nb, nl = N_ATTN_LAYERS, N_POOL_LAYERS, DEPTH
    return {
        "x": jax.random.normal(ks[0], (BATCH, SEQ, D_MODEL), f32),
        "positions": jnp.broadcast_to(jnp.arange(SEQ, dtype=jnp.int32), (BATCH, SEQ)),
        "ln_mix": gain(ks[1], (nl, D_MODEL)),
        "ln_mlp": gain(ks[2], (nl, D_MODEL)),
        "w_dqkv": nrm(ks[3], (na, D_MODEL, Q_LORA + KV_LORA + QK_ROPE), D_MODEL),
        "q_norm": gain(ks[4], (na, Q_LORA)),
        "kv_norm": gain(ks[5], (na, KV_LORA)),
        "w_uq": nrm(ks[6], (na, Q_LORA, N_HEADS * QK_HEAD), Q_LORA),
        "w_ukv": nrm(ks[7], (na, KV_LORA, N_HEADS * (QK_NOPE + V_HEAD)), KV_LORA),
        "w_o": nrm(ks[8], (na, N_HEADS * V_HEAD, D_MODEL), N_HEADS * V_HEAD),
        "w_pool": nrm(ks[9], (nb, N_POOL_GROUPS, POOL_GROUP, POOL_GROUP), POOL_GROUP),
        "pool_scale": gain(ks[10], (nb, D_MODEL)),
        "w_up": nrm(ks[11], (nl, D_MODEL, D_FF), D_MODEL),
        "w_down": nrm(ks[12], (nl, D_FF, D_MODEL), D_FF),
        "final_norm": gain(ks[13], (D_MODEL,)),
    }


def reference(x, positions, ln_mix, ln_mlp, w_dqkv, q_norm, kv_norm, w_uq, w_ukv, w_o,
              w_pool, pool_scale, w_up, w_down, final_norm):
    h = x
    for i in range(DEPTH):
        hn = rmsnorm(h, ln_mix[i])
        j = i // N_MIXERS
        if i % N_MIXERS == 0:
            mix = latent_attention(hn, positions, w_dqkv[j], q_norm[j], kv_norm[j],
                                   w_uq[j], w_ukv[j], w_o[j])
        else:
            mix = multiscale_pool(hn, w_pool[j], pool_scale[j])
        h = h + mix
        h = h + sqrelu_mlp(rmsnorm(h, ln_mlp[i]), w_up[i], w_down[i])
    return rmsnorm(h, final_norm)
```

```python
import functools
import math

import jax
import jax.numpy as jnp
from jax import lax
from jax.experimental import pallas as pl
from jax.experimental.pallas import tpu as pltpu

D_MODEL = 2048
N_HEADS = 16
Q_LORA = 512
KV_LORA = 512
QK_NOPE = 128
QK_ROPE = 64
V_HEAD = 128
QK_HEAD = QK_NOPE + QK_ROPE
ROPE_THETA = 10000.0
POOL_WINDOWS = (2, 4, 8, 16)
POOL_GROUP = D_MODEL // len(POOL_WINDOWS)
D_FF = 4 * D_MODEL
EPS = 1e-6

LANES = 128
QK_PAD = 2 * LANES
POOL_HALO = 16
VMEM_LIMIT_BYTES = 56 * 1024 * 1024

F32 = jnp.float32
BF16 = jnp.bfloat16

ROW_TILE = 512
ATTN_TILE = 512
MLP_ROW_TILE = 512
MLP_FF_TILE = 1024


def _params(*semantics):
    return pltpu.CompilerParams(dimension_semantics=semantics,
                                vmem_limit_bytes=VMEM_LIMIT_BYTES)


def _rms(x, g):
    ms = jnp.mean(x * x, axis=-1, keepdims=True)
    return x * lax.rsqrt(ms + EPS) * g


def _rope_tables(pos, inv_freq):
    ang = pos.astype(F32) * inv_freq
    cos = jnp.cos(ang)
    sin = jnp.sin(ang)
    lane = lax.broadcasted_iota(jnp.int32, ang.shape, 1)
    first_half = (lane & (QK_ROPE // 2)) == 0
    return cos, jnp.where(first_half, -sin, 0.0), jnp.where(first_half, 0.0, sin)


def _rope(x, cos, sin_lo, sin_hi):
    half = QK_ROPE // 2
    return (x * cos + pltpu.roll(x, LANES - half, 1) * sin_lo
            + pltpu.roll(x, half, 1) * sin_hi)


def _attn_down_kernel(x_ref, g_ref, w_ref, qn_ref, kvn_ref, pos_ref, invf_ref,
                      cq_ref, ckv_ref, kr_ref):
    xn = _rms(x_ref[...], g_ref[...]).astype(BF16)
    c = jnp.dot(xn, w_ref[...], preferred_element_type=F32)
    cq_ref[...] = _rms(c[:, :Q_LORA], qn_ref[...]).astype(BF16)
    ckv_ref[...] = _rms(c[:, Q_LORA:Q_LORA + KV_LORA], kvn_ref[...]).astype(BF16)
    cos, sin_lo, sin_hi = _rope_tables(pos_ref[...], invf_ref[...])
    kr_ref[...] = _rope(c[:, Q_LORA + KV_LORA:], cos, sin_lo, sin_hi).astype(BF16)


def _attn_down(x, g, w, qn, kvn, pos, invf):
    t = x.shape[0]
    n = w.shape[1]
    row = lambda i: (i, 0)
    fixed = lambda i: (0, 0)
    return pl.pallas_call(
        _attn_down_kernel,
        grid=(t // ROW_TILE,),
        in_specs=[
            pl.BlockSpec((ROW_TILE, D_MODEL), row),
            pl.BlockSpec((1, D_MODEL), fixed),
            pl.BlockSpec((D_MODEL, n), fixed),
            pl.BlockSpec((1, Q_LORA), fixed),
            pl.BlockSpec((1, KV_LORA), fixed),
            pl.BlockSpec((ROW_TILE, LANES), row),
            pl.BlockSpec((1, LANES), fixed),
        ],
        out_specs=[
            pl.BlockSpec((ROW_TILE, Q_LORA), row),
            pl.BlockSpec((ROW_TILE, KV_LORA), row),
            pl.BlockSpec((ROW_TILE, LANES), row),
        ],
        out_shape=[
            jax.ShapeDtypeStruct((t, Q_LORA), BF16),
            jax.ShapeDtypeStruct((t, KV_LORA), BF16),
            jax.ShapeDtypeStruct((t, LANES), BF16),
        ],
        compiler_params=_params("parallel"),
        name="attn_down",
    )(x, g, w, qn, kvn, pos, invf)


def _attn_up_kernel(cq_ref, ckv_ref, kr_ref, pos_ref, invf_ref, wq_ref, wk_ref, wvt_ref,
                    q_ref, k_ref, vt_ref, *, scale):
    cos, sin_lo, sin_hi = _rope_tables(pos_ref[...], invf_ref[...])
    cq = cq_ref[...]
    ckv = ckv_ref[...]
    kr = kr_ref[...]
    for h in range(N_HEADS):
        lo = h * QK_PAD
        qh = jnp.dot(cq, wq_ref[:, lo:lo + QK_PAD], preferred_element_type=F32)
        q_ref[:, lo:lo + QK_NOPE] = (qh[:, :QK_NOPE] * scale).astype(BF16)
        q_ref[:, lo + QK_NOPE:lo + QK_PAD] = (
            _rope(qh[:, QK_NOPE:], cos, sin_lo, sin_hi) * scale).astype(BF16)
        k_ref[:, lo + QK_NOPE:lo + QK_PAD] = kr
    for p in range(N_HEADS // 2):
        kk = jnp.dot(ckv, wk_ref[:, p * QK_PAD:(p + 1) * QK_PAD], preferred_element_type=F32)
        k_ref[:, 2 * p * QK_PAD:2 * p * QK_PAD + QK_NOPE] = kk[:, :QK_NOPE].astype(BF16)
        k_ref[:, (2 * p + 1) * QK_PAD:(2 * p + 1) * QK_PAD + QK_NOPE] = kk[:, QK_NOPE:].astype(BF16)
    rows = 4 * V_HEAD
    for c in range(N_HEADS * V_HEAD // rows):
        vt = lax.dot_general(wvt_ref[c * rows:(c + 1) * rows, :], ckv,
                             (((1,), (1,)), ((), ())), preferred_element_type=F32)
        vt_ref[c * rows:(c + 1) * rows, :] = vt.astype(BF16)


def _attn_up(cq, ckv, kr, pos, invf, wq, wk, wvt, *, batch, seq, scale):
    t = cq.shape[0]
    tiles_per_seq = seq // ATTN_TILE
    row = lambda i: (i, 0)
    fixed = lambda i: (0, 0)
    return pl.pallas_call(
        functools.partial(_attn_up_kernel, scale=scale),
        grid=(t // ATTN_TILE,),
        in_specs=[
            pl.BlockSpec((ATTN_TILE, Q_LORA), row),
            pl.BlockSpec((ATTN_TILE, KV_LORA), row),
            pl.BlockSpec((ATTN_TILE, LANES), row),
            pl.BlockSpec((ATTN_TILE, LANES), row),
            pl.BlockSpec((1, LANES), fixed),
            pl.BlockSpec(wq.shape, fixed),
            pl.BlockSpec(wk.shape, fixed),
            pl.BlockSpec(wvt.shape, fixed),
        ],
        out_specs=[
            pl.BlockSpec((ATTN_TILE, N_HEADS * QK_PAD), row),
            pl.BlockSpec((ATTN_TILE, N_HEADS * QK_PAD), row),
            pl.BlockSpec((None, None, N_HEADS * V_HEAD, ATTN_TILE),
                         lambda i: (i // tiles_per_seq, i % tiles_per_seq, 0, 0)),
        ],
        out_shape=[
            jax.ShapeDtypeStruct((t, N_HEADS * QK_PAD), BF16),
            jax.ShapeDtypeStruct((t, N_HEADS * QK_PAD), BF16),
            jax.ShapeDtypeStruct((batch, tiles_per_seq, N_HEADS * V_HEAD, ATTN_TILE), BF16),
        ],
        compiler_params=_params("parallel"),
        name="attn_up",
    )(cq, ckv, kr, pos, invf, wq, wk, wvt)


def _flash_kernel(q_ref, k_ref, vt_ref, o_ref, m_sc, l_sc, acc_sc, *, seq):
    tile = ATTN_TILE

    def chunk(q, j, masked):
        k0 = pl.multiple_of(j * tile, tile)
        k = k_ref[pl.ds(k0, tile), :]
        s = lax.dot_general(k, q, (((1,), (1,)), ((), ())), preferred_element_type=F32)
        if masked:
            kpos = lax.broadcasted_iota(jnp.int32, s.shape, 0)
            qpos = lax.broadcasted_iota(jnp.int32, s.shape, 1)
            s = jnp.where(kpos <= qpos, s, -jnp.inf)
        m_prev = m_sc[...]
        m_new = jnp.maximum(m_prev, jnp.max(s, axis=0, keepdims=True))
        alpha = jnp.exp(m_prev - m_new)
        p = jnp.exp(s - m_new)
        l_sc[...] = alpha * l_sc[...] + jnp.sum(p, axis=0, keepdims=True)
        acc_sc[...] = alpha * acc_sc[...] + jnp.dot(
            vt_ref[j], p.astype(BF16), preferred_element_type=F32)
        m_sc[...] = m_new

    def q_tile(i, carry):
        q0 = pl.multiple_of(i * tile, tile)
        q = q_ref[pl.ds(q0, tile), :]
        m_sc[...] = jnp.full(m_sc.shape, -jnp.inf, F32)
        l_sc[...] = jnp.zeros(l_sc.shape, F32)
        acc_sc[...] = jnp.zeros(acc_sc.shape, F32)

        def unmasked(j, c):
            chunk(q, j, False)
            return c

        lax.fori_loop(0, i, unmasked, 0)
        chunk(q, i, True)
        o = acc_sc[...] * (1.0 / l_sc[...])
        o_ref[pl.ds(q0, tile), :] = o.T.astype(BF16)
        return carry

    lax.fori_loop(0, seq // tile, q_tile, 0)


def _flash(q, k, vt, *, batch, seq):
    t = q.shape[0]
    n_chunks = seq // ATTN_TILE
    return pl.pallas_call(
        functools.partial(_flash_kernel, seq=seq),
        grid=(batch, N_HEADS),
        in_specs=[
            pl.BlockSpec((seq, QK_PAD), lambda b, h: (b, h)),
            pl.BlockSpec((seq, QK_PAD), lambda b, h: (b, h)),
            pl.BlockSpec((None, n_chunks, V_HEAD, ATTN_TILE), lambda b, h: (b, 0, h, 0)),
        ],
        out_specs=pl.BlockSpec((seq, V_HEAD), lambda b, h: (b, h)),
        out_shape=jax.ShapeDtypeStruct((t, N_HEADS * V_HEAD), BF16),
        scratch_shapes=[
            pltpu.VMEM((1, ATTN_TILE), F32),
            pltpu.VMEM((1, ATTN_TILE), F32),
            pltpu.VMEM((V_HEAD, ATTN_TILE), F32),
        ],
        compiler_params=_params("parallel", "parallel"),
        name="flash_attn",
    )(q, k, vt)


def _out_proj_kernel(x_ref, o_ref, w_ref, h_ref):
    h_ref[...] = x_ref[...] + jnp.dot(o_ref[...], w_ref[...], preferred_element_type=F32)


def _out_proj(x, o, w):
    t = x.shape[0]
    row = lambda i: (i, 0)
    return pl.pallas_call(
        _out_proj_kernel,
        grid=(t // ROW_TILE,),
        in_specs=[
            pl.BlockSpec((ROW_TILE, D_MODEL), row),
            pl.BlockSpec((ROW_TILE, N_HEADS * V_HEAD), row),
            pl.BlockSpec(w.shape, lambda i: (0, 0)),
        ],
        out_specs=pl.BlockSpec((ROW_TILE, D_MODEL), row),
        out_shape=jax.ShapeDtypeStruct((t, D_MODEL), F32),
        compiler_params=_params("parallel"),
        name="out_proj",
    )(x, o, w)


def _mlp_kernel(x_ref, g_ref, wu_ref, wd_ref, fg_ref, o_ref, xn_sc, *, final_norm):
    j = pl.program_id(1)

    @pl.when(j == 0)
    def _():
        x = x_ref[...]
        xn_sc[...] = _rms(x, g_ref[...]).astype(BF16)
        o_ref[...] = x

    u = jnp.dot(xn_sc[...], wu_ref[...], preferred_element_type=F32)
    u = jnp.maximum(u, 0.0)
    o_ref[...] += jnp.dot((u * u).astype(BF16), wd_ref[...], preferred_element_type=F32)

    if final_norm:
        @pl.when(j == pl.num_programs(1) - 1)
        def _():
            o_ref[...] = _rms(o_ref[...], fg_ref[...])


def _mlp(x, g, wu, wd, fg, *, final_norm):
    t = x.shape[0]
    row = lambda i, j: (i, 0)
    fixed = lambda i, j: (0, 0)
    return pl.pallas_call(
        functools.partial(_mlp_kernel, final_norm=final_norm),
        grid=(t // MLP_ROW_TILE, D_FF // MLP_FF_TILE),
        in_specs=[
            pl.BlockSpec((MLP_ROW_TILE, D_MODEL), row),
            pl.BlockSpec((1, D_MODEL), fixed),
            pl.BlockSpec((D_MODEL, MLP_FF_TILE), lambda i, j: (0, j)),
            pl.BlockSpec((MLP_FF_TILE, D_MODEL), lambda i, j: (j, 0)),
            pl.BlockSpec((1, D_MODEL), fixed),
        ],
        out_specs=pl.BlockSpec((MLP_ROW_TILE, D_MODEL), row),
        out_shape=jax.ShapeDtypeStruct((t, D_MODEL), F32),
        scratch_shapes=[pltpu.VMEM((MLP_ROW_TILE, D_MODEL), BF16)],
        compiler_params=_params("parallel", "arbitrary"),
        name="mlp_final" if final_norm else "mlp",
    )(x, g, wu, wd, fg)


def _pool_mix_kernel(x_ref, halo_ref, g_ref, w_ref, ps_ref, o_ref, buf_sc, *, tiles_per_seq):
    tile = ROW_TILE
    i = pl.program_id(0)
    x = x_ref[...]
    g = g_ref[...]
    xn = _rms(x, g)
    t_in_seq = i % tiles_per_seq
    halo = jnp.where(t_in_seq == 0, 0.0, _rms(halo_ref[...], g))
    buf_sc[:POOL_HALO, :] = halo
    buf_sc[POOL_HALO:, :] = xn
    t = t_in_seq * tile + lax.broadcasted_iota(jnp.int32, (tile, 1), 0)
    for gi, win in enumerate(POOL_WINDOWS):
        cols = slice(gi * POOL_GROUP, (gi + 1) * POOL_GROUP)
        xg = xn[:, cols]
        acc = xg
        for d in range(1, win):
            acc = acc + buf_sc[POOL_HALO - d:POOL_HALO - d + tile, cols]
        cnt = jnp.minimum(t + 1, win).astype(F32)
        y = (acc / cnt - xg).astype(BF16)
        mix = jnp.dot(y, w_ref[gi], preferred_element_type=F32)
        o_ref[:, cols] = x[:, cols] + mix * ps_ref[:, cols]


def _pool_mix(x, g, w, ps, *, seq):
    t = x.shape[0]
    tiles_per_seq = seq // ROW_TILE
    halo_blocks = ROW_TILE // POOL_HALO
    row = lambda i: (i, 0)
    fixed = lambda i: (0, 0)
    return pl.pallas_call(
        functools.partial(_pool_mix_kernel, tiles_per_seq=tiles_per_seq),
        grid=(t // ROW_TILE,),
        in_specs=[
            pl.BlockSpec((ROW_TILE, D_MODEL), row),
            pl.BlockSpec((POOL_HALO, D_MODEL), lambda i: (jnp.maximum(i * halo_blocks - 1, 0), 0)),
            pl.BlockSpec((1, D_MODEL), fixed),
            pl.BlockSpec(w.shape, lambda i: (0, 0, 0)),
            pl.BlockSpec((1, D_MODEL), fixed),
        ],
        out_specs=pl.BlockSpec((ROW_TILE, D_MODEL), row),
        out_shape=jax.ShapeDtypeStruct((t, D_MODEL), F32),
        scratch_shapes=[pltpu.VMEM((ROW_TILE + POOL_HALO, D_MODEL), F32)],
        compiler_params=_params("parallel"),
        name="pool_mix",
    )(x, x, g, w, ps)


def _per_head_pad(w, real):
    k = w.shape[0]
    w = w.reshape(k, N_HEADS, real)
    return jnp.pad(w, ((0, 0), (0, 0), (0, LANES - real))).reshape(k, N_HEADS * LANES)


def kernel(x, positions, ln_mix, ln_mlp, w_dqkv, q_norm, kv_norm, w_uq, w_ukv, w_o,
           w_pool, pool_scale, w_up, w_down, final_norm):
    batch, seq, d = x.shape
    assert d == D_MODEL and seq % ATTN_TILE == 0 and seq % ROW_TILE == 0
    t = batch * seq
    h = x.reshape(t, d)

    w_dqkv_p = jnp.pad(w_dqkv[0], ((0, 0), (0, LANES - QK_ROPE))).astype(BF16)
    wq = w_uq[0].reshape(Q_LORA, N_HEADS, QK_HEAD)
    wq = jnp.concatenate(
        [wq[:, :, :QK_NOPE], jnp.pad(wq[:, :, QK_NOPE:], ((0, 0), (0, 0), (0, LANES - QK_ROPE)))],
        axis=-1).reshape(Q_LORA, N_HEADS * QK_PAD).astype(BF16)
    wkv = w_ukv[0].reshape(KV_LORA, N_HEADS, QK_NOPE + V_HEAD)
    wk = wkv[:, :, :QK_NOPE].reshape(KV_LORA, N_HEADS * QK_NOPE).astype(BF16)
    wvt = wkv[:, :, QK_NOPE:].reshape(KV_LORA, N_HEADS * V_HEAD).T.astype(BF16)
    wo = w_o[0].astype(BF16)
    wp = w_pool[0].astype(BF16)
    wu = w_up.astype(BF16)
    wd = w_down.astype(BF16)

    inv_freq = 1.0 / (ROPE_THETA ** (jnp.arange(0, QK_ROPE, 2, dtype=F32) / QK_ROPE))
    invf = jnp.concatenate([inv_freq, inv_freq, jnp.zeros((LANES - QK_ROPE,), F32)]).reshape(1, LANES)
    pos = jnp.broadcast_to(positions.reshape(t, 1), (t, LANES))
    scale = 1.0 / math.sqrt(QK_HEAD)

    row = lambda v: v.reshape(1, -1)

    cq, ckv, kr = _attn_down(h, row(ln_mix[0]), w_dqkv_p, row(q_norm[0]), row(kv_norm[0]), pos, invf)
    q, k, vt = _attn_up(cq, ckv, kr, pos, invf, wq, wk, wvt, batch=batch, seq=seq, scale=scale)
    o = _flash(q, k, vt, batch=batch, seq=seq)
    h = _out_proj(h, o, wo)
    h = _mlp(h, row(ln_mlp[0]), wu[0], wd[0], row(final_norm), final_norm=False)

    h = _pool_mix(h, row(ln_mix[1]), wp, row(pool_scale[0]), seq=seq)
    h = _mlp(h, row(ln_mlp[1]), wu[1], wd[1], row(final_norm), final_norm=True)
    return h.reshape(batch, seq, d)
```

```python
import functools
import math

import jax
import jax.numpy as jnp
from jax import lax
from jax.experimental import pallas as pl
from jax.experimental.pallas import tpu as pltpu

D_MODEL = 2048
N_HEADS = 16
Q_LORA = 512
KV_LORA = 512
QK_NOPE = 128
QK_ROPE = 64
V_HEAD = 128
QK_HEAD = QK_NOPE + QK_ROPE
ROPE_THETA = 10000.0
POOL_WINDOWS = (2, 4, 8, 16)
POOL_GROUP = D_MODEL // len(POOL_WINDOWS)
D_FF = 4 * D_MODEL
EPS = 1e-6

LANES = 128
QK_PAD = 2 * LANES
POOL_HALO = 16
VMEM_LIMIT_BYTES = 56 * 1024 * 1024

F32 = jnp.float32
BF16 = jnp.bfloat16

ROW_TILE = 512
ATTN_TILE = 512
MLP_ROW_TILE = 512
MLP_FF_TILE = 1024


def _params(*semantics):
    return pltpu.CompilerParams(dimension_semantics=semantics,
                                vmem_limit_bytes=VMEM_LIMIT_BYTES)


def _rms(x, g):
    ms = jnp.mean(x * x, axis=-1, keepdims=True)
    return x * lax.rsqrt(ms + EPS) * g


def _rope_tables(pos, inv_freq):
    ang = pos.astype(F32) * inv_freq
    cos = jnp.cos(ang)
    sin = jnp.sin(ang)
    lane = lax.broadcasted_iota(jnp.int32, ang.shape, 1)
    first_half = (lane & (QK_ROPE // 2)) == 0
    return cos, jnp.where(first_half, -sin, 0.0), jnp.where(first_half, 0.0, sin)


def _rope(x, cos, sin_lo, sin_hi):
    half = QK_ROPE // 2
    return (x * cos + pltpu.roll(x, LANES - half, 1) * sin_lo
            + pltpu.roll(x, half, 1) * sin_hi)


def _attn_down_kernel(x_ref, g_ref, w_ref, qn_ref, kvn_ref, pos_ref, invf_ref,
                      cq_ref, ckv_ref, kr_ref):
    xn = _rms(x_ref[...], g_ref[...]).astype(BF16)
    c = jnp.dot(xn, w_ref[...], preferred_element_type=F32)
    cq_ref[...] = _rms(c[:, :Q_LORA], qn_ref[...]).astype(BF16)
    ckv_ref[...] = _rms(c[:, Q_LORA:Q_LORA + KV_LORA], kvn_ref[...]).astype(BF16)
    cos, sin_lo, sin_hi = _rope_tables(pos_ref[...], invf_ref[...])
    kr_ref[...] = _rope(c[:, Q_LORA + KV_LORA:], cos, sin_lo, sin_hi).astype(BF16)


def _attn_down(x, g, w, qn, kvn, pos, invf):
    t = x.shape[0]
    n = w.shape[1]
    row = lambda i: (i, 0)
    fixed = lambda i: (0, 0)
    return pl.pallas_call(
        _attn_down_kernel,
        grid=(t // ROW_TILE,),
        in_specs=[
            pl.BlockSpec((ROW_TILE, D_MODEL), row),
            pl.BlockSpec((1, D_MODEL), fixed),
            pl.BlockSpec((D_MODEL, n), fixed),
            pl.BlockSpec((1, Q_LORA), fixed),
            pl.BlockSpec((1, KV_LORA), fixed),
            pl.BlockSpec((ROW_TILE, LANES), row),
            pl.BlockSpec((1, LANES), fixed),
        ],
        out_specs=[
            pl.BlockSpec((ROW_TILE, Q_LORA), row),
            pl.BlockSpec((ROW_TILE, KV_LORA), row),
            pl.BlockSpec((ROW_TILE, LANES), row),
        ],
        out_shape=[
            jax.ShapeDtypeStruct((t, Q_LORA), BF16),
            jax.ShapeDtypeStruct((t, KV_LORA), BF16),
            jax.ShapeDtypeStruct((t, LANES), BF16),
        ],
        compiler_params=_params("parallel"),
        name="attn_down",
    )(x, g, w, qn, kvn, pos, invf)


def _attn_up_kernel(cq_ref, ckv_ref, kr_ref, pos_ref, invf_ref, wq_ref, wk_ref, wvt_ref,
                    q_ref, k_ref, vt_ref, *, scale):
    cos, sin_lo, sin_hi = _rope_tables(pos_ref[...], invf_ref[...])
    cq = cq_ref[...]
    ckv = ckv_ref[...]
    kr = kr_ref[...]
    for h in range(N_HEADS):
        lo = h * QK_PAD
        qh = jnp.dot(cq, wq_ref[:, lo:lo + QK_PAD], preferred_element_type=F32)
        q_ref[:, lo:lo + QK_NOPE] = (qh[:, :QK_NOPE] * scale).astype(BF16)
        q_ref[:, lo + QK_NOPE:lo + QK_PAD] = (
            _rope(qh[:, QK_NOPE:], cos, sin_lo, sin_hi) * scale).astype(BF16)
        k_ref[:, lo + QK_NOPE:lo + QK_PAD] = kr
    for p in range(N_HEADS // 2):
        kk = jnp.dot(ckv, wk_ref[:, p * QK_PAD:(p + 1) * QK_PAD], preferred_element_type=F32)
        k_ref[:, 2 * p * QK_PAD:2 * p * QK_PAD + QK_NOPE] = kk[:, :QK_NOPE].astype(BF16)
        k_ref[:, (2 * p + 1) * QK_PAD:(2 * p + 1) * QK_PAD + QK_NOPE] = kk[:, QK_NOPE:].astype(BF16)
    rows = 4 * V_HEAD
    for c in range(N_HEADS * V_HEAD // rows):
        vt = lax.dot_general(wvt_ref[c * rows:(c + 1) * rows, :], ckv,
                             (((1,), (1,)), ((), ())), preferred_element_type=F32)
        vt_ref[c * rows:(c + 1) * rows, :] = vt.astype(BF16)


def _attn_up(cq, ckv, kr, pos, invf, wq, wk, wvt, *, batch, seq, scale):
    t = cq.shape[0]
    tiles_per_seq = seq // ATTN_TILE
    row = lambda i: (i, 0)
    fixed = lambda i: (0, 0)
    return pl.pallas_call(
        functools.partial(_attn_up_kernel, scale=scale),
        grid=(t // ATTN_TILE,),
        in_specs=[
            pl.BlockSpec((ATTN_TILE, Q_LORA), row),
            pl.BlockSpec((ATTN_TILE, KV_LORA), row),
            pl.BlockSpec((ATTN_TILE, LANES), row),
            pl.BlockSpec((ATTN_TILE, LANES), row),
            pl.BlockSpec((1, LANES), fixed),
            pl.BlockSpec(wq.shape, fixed),
            pl.BlockSpec(wk.shape, fixed),
            pl.BlockSpec(wvt.shape, fixed),
        ],
        out_specs=[
            pl.BlockSpec((ATTN_TILE, N_HEADS * QK_PAD), row),
            pl.BlockSpec((ATTN_TILE, N_HEADS * QK_PAD), row),
            pl.BlockSpec((None, None, N_HEADS * V_HEAD, ATTN_TILE),
                         lambda i: (i // tiles_per_seq, i % tiles_per_seq, 0, 0)),
        ],
        out_shape=[
            jax.ShapeDtypeStruct((t, N_HEADS * QK_PAD), BF16),
            jax.ShapeDtypeStruct((t, N_HEADS * QK_PAD), BF16),
            jax.ShapeDtypeStruct((batch, tiles_per_seq, N_HEADS * V_HEAD, ATTN_TILE), BF16),
        ],
        compiler_params=_params("parallel"),
        name="attn_up",
    )(cq, ckv, kr, pos, invf, wq, wk, wvt)


def _flash_kernel(q_ref, k_ref, vt_ref, o_ref, s_sc, m_sc, l_sc, acc_sc, *, n_tiles):
    tile = ATTN_TILE

    def rows(idx):
        if isinstance(idx, int):
            return pl.ds(idx * tile, tile)
        return pl.ds(pl.multiple_of(idx * tile, tile), tile)

    def scores(i, j):
        return lax.dot_general(k_ref[rows(j), :], q_ref[rows(i), :],
                               (((1,), (1,)), ((), ())), preferred_element_type=F32)

    s_sc[0] = scores(0, 0)
    for i in range(n_tiles):
        if i + 1 < n_tiles:
            s_sc[(i + 1) % 2] = scores(i + 1, i + 1)
        s = s_sc[i % 2]
        kpos = lax.broadcasted_iota(jnp.int32, s.shape, 0)
        qpos = lax.broadcasted_iota(jnp.int32, s.shape, 1)
        s = jnp.where(kpos <= qpos, s, -jnp.inf)
        m = jnp.max(s, axis=0, keepdims=True)
        p = jnp.exp2(s - m)
        m_sc[i] = m
        l_sc[i] = jnp.sum(p, axis=0, keepdims=True)
        acc_sc[i] = jnp.dot(vt_ref[i], p.astype(BF16), preferred_element_type=F32)

    def update(i, j, slot):
        s = s_sc[slot]
        m_prev = m_sc[i]
        m_new = jnp.maximum(m_prev, jnp.max(s, axis=0, keepdims=True))
        alpha = jnp.exp2(m_prev - m_new)
        p = jnp.exp2(s - m_new)
        m_sc[i] = m_new
        l_sc[i] = alpha * l_sc[i] + jnp.sum(p, axis=0, keepdims=True)
        acc_sc[i] = alpha * acc_sc[i] + jnp.dot(
            vt_ref[j], p.astype(BF16), preferred_element_type=F32)

    def succ(i, j):
        wrap = j + 1 >= i
        return jnp.where(wrap, i + 1, i), jnp.where(wrap, 0, j + 1)

    n_off = n_tiles * (n_tiles - 1) // 2
    assert n_off % 2 == 0
    s_sc[0] = scores(1, 0)

    def body(_, carry):
        i0, j0 = carry
        i1, j1 = succ(i0, j0)
        i2, j2 = succ(i1, j1)
        s_sc[1] = scores(i1, j1)
        update(i0, j0, 0)
        s_sc[0] = scores(jnp.minimum(i2, n_tiles - 1), jnp.minimum(j2, n_tiles - 2))
        update(i1, j1, 1)
        return i2, j2

    lax.fori_loop(0, n_off // 2, body, (jnp.int32(1), jnp.int32(0)))

    for i in range(n_tiles):
        o = acc_sc[i] * (1.0 / l_sc[i])
        o_ref[i * tile:(i + 1) * tile, :] = o.T.astype(BF16)


def _flash(q, k, vt, *, batch, seq):
    t = q.shape[0]
    n_tiles = seq // ATTN_TILE
    return pl.pallas_call(
        functools.partial(_flash_kernel, n_tiles=n_tiles),
        grid=(batch, N_HEADS),
        in_specs=[
            pl.BlockSpec((seq, QK_PAD), lambda b, h: (b, h)),
            pl.BlockSpec((seq, QK_PAD), lambda b, h: (b, h)),
            pl.BlockSpec((None, n_tiles, V_HEAD, ATTN_TILE), lambda b, h: (b, 0, h, 0)),
        ],
        out_specs=pl.BlockSpec((seq, V_HEAD), lambda b, h: (b, h)),
        out_shape=jax.ShapeDtypeStruct((t, N_HEADS * V_HEAD), BF16),
        scratch_shapes=[
            pltpu.VMEM((2, ATTN_TILE, ATTN_TILE), F32),
            pltpu.VMEM((n_tiles, 1, ATTN_TILE), F32),
            pltpu.VMEM((n_tiles, 1, ATTN_TILE), F32),
            pltpu.VMEM((n_tiles, V_HEAD, ATTN_TILE), F32),
        ],
        compiler_params=_params("parallel", "parallel"),
        name="flash_attn",
    )(q, k, vt)


def _out_proj_kernel(x_ref, o_ref, w_ref, h_ref):
    h_ref[...] = x_ref[...] + jnp.dot(o_ref[...], w_ref[...], preferred_element_type=F32)


def _out_proj(x, o, w):
    t = x.shape[0]
    row = lambda i: (i, 0)
    return pl.pallas_call(
        _out_proj_kernel,
        grid=(t // ROW_TILE,),
        in_specs=[
            pl.BlockSpec((ROW_TILE, D_MODEL), row),
            pl.BlockSpec((ROW_TILE, N_HEADS * V_HEAD), row),
            pl.BlockSpec(w.shape, lambda i: (0, 0)),
        ],
        out_specs=pl.BlockSpec((ROW_TILE, D_MODEL), row),
        out_shape=jax.ShapeDtypeStruct((t, D_MODEL), F32),
        compiler_params=_params("parallel"),
        name="out_proj",
    )(x, o, w)


def _mlp_kernel(x_ref, g_ref, wu_ref, wd_ref, fg_ref, o_ref, xn_sc, *, final_norm):
    j = pl.program_id(1)

    @pl.when(j == 0)
    def _():
        x = x_ref[...]
        xn_sc[...] = _rms(x, g_ref[...]).astype(BF16)
        o_ref[...] = x

    u = jnp.dot(xn_sc[...], wu_ref[...], preferred_element_type=F32)
    u = jnp.maximum(u, 0.0)
    o_ref[...] += jnp.dot((u * u).astype(BF16), wd_ref[...], preferred_element_type=F32)

    if final_norm:
        @pl.when(j == pl.num_programs(1) - 1)
        def _():
            o_ref[...] = _rms(o_ref[...], fg_ref[...])


def _mlp(x, g, wu, wd, fg, *, final_norm):
    t = x.shape[0]
    row = lambda i, j: (i, 0)
    fixed = lambda i, j: (0, 0)
    return pl.pallas_call(
        functools.partial(_mlp_kernel, final_norm=final_norm),
        grid=(t // MLP_ROW_TILE, D_FF // MLP_FF_TILE),
        in_specs=[
            pl.BlockSpec((MLP_ROW_TILE, D_MODEL), row),
            pl.BlockSpec((1, D_MODEL), fixed),
            pl.BlockSpec((D_MODEL, MLP_FF_TILE), lambda i, j: (0, j)),
            pl.BlockSpec((MLP_FF_TILE, D_MODEL), lambda i, j: (j, 0)),
            pl.BlockSpec((1, D_MODEL), fixed),
        ],
        out_specs=pl.BlockSpec((MLP_ROW_TILE, D_MODEL), row),
        out_shape=jax.ShapeDtypeStruct((t, D_MODEL), F32),
        scratch_shapes=[pltpu.VMEM((MLP_ROW_TILE, D_MODEL), BF16)],
        compiler_params=_params("parallel", "arbitrary"),
        name="mlp_final" if final_norm else "mlp",
    )(x, g, wu, wd, fg)


def _pool_mix_kernel(x_ref, halo_ref, g_ref, w_ref, ps_ref, o_ref, buf_sc, *, tiles_per_seq):
    tile = ROW_TILE
    i = pl.program_id(0)
    x = x_ref[...]
    g = g_ref[...]
    xn = _rms(x, g)
    t_in_seq = i % tiles_per_seq
    halo = jnp.where(t_in_seq == 0, 0.0, _rms(halo_ref[...], g))
    buf_sc[:POOL_HALO, :] = halo
    buf_sc[POOL_HALO:, :] = xn
    t = t_in_seq * tile + lax.broadcasted_iota(jnp.int32, (tile, 1), 0)
    for gi, win in enumerate(POOL_WINDOWS):
        cols = slice(gi * POOL_GROUP, (gi + 1) * POOL_GROUP)
        xg = xn[:, cols]
        acc = xg
        for d in range(1, win):
            acc = acc + buf_sc[POOL_HALO - d:POOL_HALO - d + tile, cols]
        cnt = jnp.minimum(t + 1, win).astype(F32)
        y = (acc / cnt - xg).astype(BF16)
        mix = jnp.dot(y, w_ref[gi], preferred_element_type=F32)
        o_ref[:, cols] = x[:, cols] + mix * ps_ref[:, cols]


def _pool_mix(x, g, w, ps, *, seq):
    t = x.shape[0]
    tiles_per_seq = seq // ROW_TILE
    halo_blocks = ROW_TILE // POOL_HALO
    row = lambda i: (i, 0)
    fixed = lambda i: (0, 0)
    return pl.pallas_call(
        functools.partial(_pool_mix_kernel, tiles_per_seq=tiles_per_seq),
        grid=(t // ROW_TILE,),
        in_specs=[
            pl.BlockSpec((ROW_TILE, D_MODEL), row),
            pl.BlockSpec((POOL_HALO, D_MODEL), lambda i: (jnp.maximum(i * halo_blocks - 1, 0), 0)),
            pl.BlockSpec((1, D_MODEL), fixed),
            pl.BlockSpec(w.shape, lambda i: (0, 0, 0)),
            pl.BlockSpec((1, D_MODEL), fixed),
        ],
        out_specs=pl.BlockSpec((ROW_TILE, D_MODEL), row),
        out_shape=jax.ShapeDtypeStruct((t, D_MODEL), F32),
        scratch_shapes=[pltpu.VMEM((ROW_TILE + POOL_HALO, D_MODEL), F32)],
        compiler_params=_params("parallel"),
        name="pool_mix",
    )(x, x, g, w, ps)


def _per_head_pad(w, real):
    k = w.shape[0]
    w = w.reshape(k, N_HEADS, real)
    return jnp.pad(w, ((0, 0), (0, 0), (0, LANES - real))).reshape(k, N_HEADS * LANES)


def kernel(x, positions, ln_mix, ln_mlp, w_dqkv, q_norm, kv_norm, w_uq, w_ukv, w_o,
           w_pool, pool_scale, w_up, w_down, final_norm):
    batch, seq, d = x.shape
    assert d == D_MODEL and seq % ATTN_TILE == 0 and seq % ROW_TILE == 0
    t = batch * seq
    h = x.reshape(t, d)

    w_dqkv_p = jnp.pad(w_dqkv[0], ((0, 0), (0, LANES - QK_ROPE))).astype(BF16)
    wq = w_uq[0].reshape(Q_LORA, N_HEADS, QK_HEAD)
    wq = jnp.concatenate(
        [wq[:, :, :QK_NOPE], jnp.pad(wq[:, :, QK_NOPE:], ((0, 0), (0, 0), (0, LANES - QK_ROPE)))],
        axis=-1).reshape(Q_LORA, N_HEADS * QK_PAD).astype(BF16)
    wkv = w_ukv[0].reshape(KV_LORA, N_HEADS, QK_NOPE + V_HEAD)
    wk = wkv[:, :, :QK_NOPE].reshape(KV_LORA, N_HEADS * QK_NOPE).astype(BF16)
    wvt = wkv[:, :, QK_NOPE:].reshape(KV_LORA, N_HEADS * V_HEAD).T.astype(BF16)
    wo = w_o[0].astype(BF16)
    wp = w_pool[0].astype(BF16)
    wu = [w_up[i].astype(BF16) for i in range(w_up.shape[0])]
    wd = [w_down[i].astype(BF16) for i in range(w_down.shape[0])]

    inv_freq = 1.0 / (ROPE_THETA ** (jnp.arange(0, QK_ROPE, 2, dtype=F32) / QK_ROPE))
    invf = jnp.concatenate([inv_freq, inv_freq, jnp.zeros((LANES - QK_ROPE,), F32)]).reshape(1, LANES)
    pos = jnp.broadcast_to(positions.reshape(t, 1), (t, LANES))
    scale = math.log2(math.e) / math.sqrt(QK_HEAD)

    row = lambda v: v.reshape(1, -1)

    cq, ckv, kr = _attn_down(h, row(ln_mix[0]), w_dqkv_p, row(q_norm[0]), row(kv_norm[0]), pos, invf)
    q, k, vt = _attn_up(cq, ckv, kr, pos, invf, wq, wk, wvt, batch=batch, seq=seq, scale=scale)
    o = _flash(q, k, vt, batch=batch, seq=seq)
    h = _out_proj(h, o, wo)
    h = _mlp(h, row(ln_mlp[0]), wu[0], wd[0], row(final_norm), final_norm=False)

    h = _pool_mix(h, row(ln_mix[1]), wp, row(pool_scale[0]), seq=seq)
    h = _mlp(h, row(ln_mlp[1]), wu[1], wd[1], row(final_norm), final_norm=True)
    return h.reshape(batch, seq, d)
```

```python
import functools
import math

import jax
import jax.numpy as jnp
from jax import lax
from jax.experimental import pallas as pl
from jax.experimental.pallas import tpu as pltpu

D_MODEL = 2048
N_HEADS = 16
Q_LORA = 512
KV_LORA = 512
QK_NOPE = 128
QK_ROPE = 64
V_HEAD = 128
QK_HEAD = QK_NOPE + QK_ROPE
ROPE_THETA = 10000.0
POOL_WINDOWS = (2, 4, 8, 16)
POOL_GROUP = D_MODEL // len(POOL_WINDOWS)
D_FF = 4 * D_MODEL
EPS = 1e-6

LANES = 128
QK_PAD = 2 * LANES
POOL_HALO = 16
VMEM_LIMIT_BYTES = 56 * 1024 * 1024

F32 = jnp.float32
BF16 = jnp.bfloat16

ROW_TILE = 512
DOWN_SUB_ROWS = 128
ATTN_TILE = 512
FLASH_UNROLL = 4
MLP_ROW_TILE = 512
MLP_FF_TILE = 1024


def _params(*semantics):
    return pltpu.CompilerParams(dimension_semantics=semantics,
                                vmem_limit_bytes=VMEM_LIMIT_BYTES)


def _rms(x, g):
    ms = jnp.mean(x * x, axis=-1, keepdims=True)
    return x * lax.rsqrt(ms + EPS) * g


def _rope_tables(pos, inv_freq):
    ang = pos.astype(F32) * inv_freq
    cos = jnp.cos(ang)
    sin = jnp.sin(ang)
    lane = lax.broadcasted_iota(jnp.int32, ang.shape, 1)
    first_half = (lane & (QK_ROPE // 2)) == 0
    return cos, jnp.where(first_half, -sin, 0.0), jnp.where(first_half, 0.0, sin)


def _rope(x, cos, sin_lo, sin_hi):
    half = QK_ROPE // 2
    return (x * cos + pltpu.roll(x, LANES - half, 1) * sin_lo
            + pltpu.roll(x, half, 1) * sin_hi)


def _attn_down_kernel(x_ref, g_ref, w_ref, qn_ref, kvn_ref, pos_ref, invf_ref,
                      cq_ref, ckv_ref, kr_ref):
    for r in range(ROW_TILE // DOWN_SUB_ROWS):
        rs = slice(r * DOWN_SUB_ROWS, (r + 1) * DOWN_SUB_ROWS)
        xn = _rms(x_ref[rs, :], g_ref[...]).astype(BF16)
        c = jnp.dot(xn, w_ref[...], preferred_element_type=F32)
        cq_ref[rs, :] = _rms(c[:, :Q_LORA], qn_ref[...]).astype(BF16)
        ckv_ref[rs, :] = _rms(c[:, Q_LORA:Q_LORA + KV_LORA], kvn_ref[...]).astype(BF16)
        cos, sin_lo, sin_hi = _rope_tables(pos_ref[rs, :], invf_ref[...])
        kr_ref[rs, :] = _rope(c[:, Q_LORA + KV_LORA:], cos, sin_lo, sin_hi).astype(BF16)


def _attn_down(x, g, w, qn, kvn, pos, invf):
    t = x.shape[0]
    n = w.shape[1]
    row = lambda i: (i, 0)
    fixed = lambda i: (0, 0)
    return pl.pallas_call(
        _attn_down_kernel,
        grid=(t // ROW_TILE,),
        in_specs=[
            pl.BlockSpec((ROW_TILE, D_MODEL), row),
            pl.BlockSpec((1, D_MODEL), fixed),
            pl.BlockSpec((D_MODEL, n), fixed),
            pl.BlockSpec((1, Q_LORA), fixed),
            pl.BlockSpec((1, KV_LORA), fixed),
            pl.BlockSpec((ROW_TILE, LANES), row),
            pl.BlockSpec((1, LANES), fixed),
        ],
        out_specs=[
            pl.BlockSpec((ROW_TILE, Q_LORA), row),
            pl.BlockSpec((ROW_TILE, KV_LORA), row),
            pl.BlockSpec((ROW_TILE, LANES), row),
        ],
        out_shape=[
            jax.ShapeDtypeStruct((t, Q_LORA), BF16),
            jax.ShapeDtypeStruct((t, KV_LORA), BF16),
            jax.ShapeDtypeStruct((t, LANES), BF16),
        ],
        compiler_params=_params("parallel"),
        name="attn_down",
    )(x, g, w, qn, kvn, pos, invf)


def _attn_up_kernel(cq_ref, ckv_ref, kr_ref, pos_ref, invf_ref, wq_ref, wk_ref, wvt_ref,
                    q_ref, k_ref, vt_ref, *, scale):
    cos, sin_lo, sin_hi = _rope_tables(pos_ref[...], invf_ref[...])
    cq = cq_ref[...]
    ckv = ckv_ref[...]
    kr = kr_ref[...]
    for h in range(N_HEADS):
        lo = h * QK_PAD
        qh = jnp.dot(cq, wq_ref[:, lo:lo + QK_PAD], preferred_element_type=F32)
        q_ref[h, :, :QK_NOPE] = (qh[:, :QK_NOPE] * scale).astype(BF16)
        q_ref[h, :, QK_NOPE:] = (_rope(qh[:, QK_NOPE:], cos, sin_lo, sin_hi) * scale).astype(BF16)
        k_ref[h, :, QK_NOPE:] = kr
    for p in range(N_HEADS // 2):
        kk = jnp.dot(ckv, wk_ref[:, p * QK_PAD:(p + 1) * QK_PAD], preferred_element_type=F32)
        k_ref[2 * p, :, :QK_NOPE] = kk[:, :QK_NOPE].astype(BF16)
        k_ref[2 * p + 1, :, :QK_NOPE] = kk[:, QK_NOPE:].astype(BF16)
    rows = 4 * V_HEAD
    for c in range(N_HEADS * V_HEAD // rows):
        vt = lax.dot_general(wvt_ref[c * rows:(c + 1) * rows, :], ckv,
                             (((1,), (1,)), ((), ())), preferred_element_type=F32)
        vt_ref[c * rows:(c + 1) * rows, :] = vt.astype(BF16)


def _attn_up(cq, ckv, kr, pos, invf, wq, wk, wvt, *, batch, seq, scale):
    t = cq.shape[0]
    tiles_per_seq = seq // ATTN_TILE
    row = lambda i: (i, 0)
    fixed = lambda i: (0, 0)
    head_major = lambda i: (i // tiles_per_seq, 0, i % tiles_per_seq, 0)
    return pl.pallas_call(
        functools.partial(_attn_up_kernel, scale=scale),
        grid=(t // ATTN_TILE,),
        in_specs=[
            pl.BlockSpec((ATTN_TILE, Q_LORA), row),
            pl.BlockSpec((ATTN_TILE, KV_LORA), row),
            pl.BlockSpec((ATTN_TILE, LANES), row),
            pl.BlockSpec((ATTN_TILE, LANES), row),
            pl.BlockSpec((1, LANES), fixed),
            pl.BlockSpec(wq.shape, fixed),
            pl.BlockSpec(wk.shape, fixed),
            pl.BlockSpec(wvt.shape, fixed),
        ],
        out_specs=[
            pl.BlockSpec((None, N_HEADS, ATTN_TILE, QK_PAD), head_major),
            pl.BlockSpec((None, N_HEADS, ATTN_TILE, QK_PAD), head_major),
            pl.BlockSpec((None, None, N_HEADS * V_HEAD, ATTN_TILE),
                         lambda i: (i // tiles_per_seq, i % tiles_per_seq, 0, 0)),
        ],
        out_shape=[
            jax.ShapeDtypeStruct((batch, N_HEADS, seq, QK_PAD), BF16),
            jax.ShapeDtypeStruct((batch, N_HEADS, seq, QK_PAD), BF16),
            jax.ShapeDtypeStruct((batch, tiles_per_seq, N_HEADS * V_HEAD, ATTN_TILE), BF16),
        ],
        compiler_params=_params("parallel"),
        name="attn_up",
    )(cq, ckv, kr, pos, invf, wq, wk, wvt)


def _flash_kernel(q_ref, k_ref, vt_ref, o_ref, s_sc, m_sc, l_sc, acc_sc, *, n_tiles):
    tile = ATTN_TILE

    def rows(idx):
        if isinstance(idx, int):
            return pl.ds(idx * tile, tile)
        return pl.ds(pl.multiple_of(idx * tile, tile), tile)

    def scores(i, j):
        return lax.dot_general(k_ref[rows(j), :], q_ref[rows(i), :],
                               (((1,), (1,)), ((), ())), preferred_element_type=F32)

    s_sc[0] = scores(0, 0)
    for i in range(n_tiles):
        if i + 1 < n_tiles:
            s_sc[(i + 1) % 2] = scores(i + 1, i + 1)
        s = s_sc[i % 2]
        kpos = lax.broadcasted_iota(jnp.int32, s.shape, 0)
        qpos = lax.broadcasted_iota(jnp.int32, s.shape, 1)
        s = jnp.where(kpos <= qpos, s, -jnp.inf)
        m = jnp.max(s, axis=0, keepdims=True)
        p = jnp.exp2(s - m)
        m_sc[i] = m
        l_sc[i] = jnp.sum(p, axis=0, keepdims=True)
        acc_sc[i] = jnp.dot(vt_ref[i], p.astype(BF16), preferred_element_type=F32)

    def update(i, j, slot):
        s = s_sc[slot]
        m_prev = m_sc[i]
        m_new = jnp.maximum(m_prev, jnp.max(s, axis=0, keepdims=True))
        alpha = jnp.exp2(m_prev - m_new)
        p = jnp.exp2(s - m_new)
        m_sc[i] = m_new
        l_sc[i] = alpha * l_sc[i] + jnp.sum(p, axis=0, keepdims=True)
        acc_sc[i] = alpha * acc_sc[i] + jnp.dot(
            vt_ref[j], p.astype(BF16), preferred_element_type=F32)

    def succ(i, j):
        wrap = j + 1 >= i
        return jnp.where(wrap, i + 1, i), jnp.where(wrap, 0, j + 1)

    n_off = n_tiles * (n_tiles - 1) // 2
    assert n_off % FLASH_UNROLL == 0 and FLASH_UNROLL % 2 == 0
    s_sc[0] = scores(1, 0)

    def body(_, cur):
        for u in range(FLASH_UNROLL):
            nxt = succ(*cur)
            s_sc[1 - u % 2] = scores(jnp.minimum(nxt[0], n_tiles - 1),
                                     jnp.minimum(nxt[1], n_tiles - 2))
            update(*cur, u % 2)
            cur = nxt
        return cur

    lax.fori_loop(0, n_off // FLASH_UNROLL, body, (jnp.int32(1), jnp.int32(0)))

    for i in range(n_tiles):
        o = acc_sc[i] * (1.0 / l_sc[i])
        o_ref[i * tile:(i + 1) * tile, :] = o.T.astype(BF16)


def _flash(q, k, vt, *, batch, seq):
    n_tiles = seq // ATTN_TILE
    return pl.pallas_call(
        functools.partial(_flash_kernel, n_tiles=n_tiles),
        grid=(batch, N_HEADS),
        in_specs=[
            pl.BlockSpec((None, None, seq, QK_PAD), lambda b, h: (b, h, 0, 0)),
            pl.BlockSpec((None, None, seq, QK_PAD), lambda b, h: (b, h, 0, 0)),
            pl.BlockSpec((None, n_tiles, V_HEAD, ATTN_TILE), lambda b, h: (b, 0, h, 0)),
        ],
        out_specs=pl.BlockSpec((None, None, seq, V_HEAD), lambda b, h: (b, h, 0, 0)),
        out_shape=jax.ShapeDtypeStruct((batch, N_HEADS, seq, V_HEAD), BF16),
        scratch_shapes=[
            pltpu.VMEM((2, ATTN_TILE, ATTN_TILE), F32),
            pltpu.VMEM((n_tiles, 1, ATTN_TILE), F32),
            pltpu.VMEM((n_tiles, 1, ATTN_TILE), F32),
            pltpu.VMEM((n_tiles, V_HEAD, ATTN_TILE), F32),
        ],
        compiler_params=_params("parallel", "parallel"),
        name="flash_attn",
    )(q, k, vt)


def _out_proj_kernel(x_ref, o_ref, w_ref, h_ref):
    o = jnp.concatenate([o_ref[h] for h in range(N_HEADS)], axis=-1)
    h_ref[...] = x_ref[...] + jnp.dot(o, w_ref[...], preferred_element_type=F32)


def _out_proj(x, o, w, *, seq):
    t = x.shape[0]
    tiles_per_seq = seq // ROW_TILE
    row = lambda i: (i, 0)
    return pl.pallas_call(
        _out_proj_kernel,
        grid=(t // ROW_TILE,),
        in_specs=[
            pl.BlockSpec((ROW_TILE, D_MODEL), row),
            pl.BlockSpec((None, N_HEADS, ROW_TILE, V_HEAD),
                         lambda i: (i // tiles_per_seq, 0, i % tiles_per_seq, 0)),
            pl.BlockSpec(w.shape, lambda i: (0, 0)),
        ],
        out_specs=pl.BlockSpec((ROW_TILE, D_MODEL), row),
        out_shape=jax.ShapeDtypeStruct((t, D_MODEL), F32),
        compiler_params=_params("parallel"),
        name="out_proj",
    )(x, o, w)


def _mlp_kernel(x_ref, g_ref, wu_ref, wd_ref, fg_ref, o_ref, xn_sc, *, final_norm):
    j = pl.program_id(1)

    @pl.when(j == 0)
    def _():
        x = x_ref[...]
        xn_sc[...] = _rms(x, g_ref[...]).astype(BF16)
        o_ref[...] = x

    u = jnp.dot(xn_sc[...], wu_ref[...], preferred_element_type=F32)
    u = jnp.maximum(u, 0.0)
    o_ref[...] += jnp.dot((u * u).astype(BF16), wd_ref[...], preferred_element_type=F32)

    if final_norm:
        @pl.when(j == pl.num_programs(1) - 1)
        def _():
            o_ref[...] = _rms(o_ref[...], fg_ref[...])


def _mlp(x, g, wu, wd, fg, *, layer, final_norm):
    t = x.shape[0]
    row = lambda i, j: (i, 0)
    fixed = lambda i, j: (0, 0)
    return pl.pallas_call(
        functools.partial(_mlp_kernel, final_norm=final_norm),
        grid=(t // MLP_ROW_TILE, D_FF // MLP_FF_TILE),
        in_specs=[
            pl.BlockSpec((MLP_ROW_TILE, D_MODEL), row),
            pl.BlockSpec((1, D_MODEL), fixed),
            pl.BlockSpec((None, D_MODEL, MLP_FF_TILE), lambda i, j: (layer, 0, j)),
            pl.BlockSpec((None, MLP_FF_TILE, D_MODEL), lambda i, j: (layer, j, 0)),
            pl.BlockSpec((1, D_MODEL), fixed),
        ],
        out_specs=pl.BlockSpec((MLP_ROW_TILE, D_MODEL), row),
        out_shape=jax.ShapeDtypeStruct((t, D_MODEL), F32),
        scratch_shapes=[pltpu.VMEM((MLP_ROW_TILE, D_MODEL), BF16)],
        compiler_params=_params("parallel", "arbitrary"),
        name="mlp_final" if final_norm else "mlp",
    )(x, g, wu, wd, fg)


def _pool_mix_kernel(x_ref, halo_ref, g_ref, w_ref, ps_ref, o_ref, buf_sc, *, tiles_per_seq):
    tile = ROW_TILE
    i = pl.program_id(0)
    x = x_ref[...]
    g = g_ref[...]
    xn = _rms(x, g)
    t_in_seq = i % tiles_per_seq
    halo = jnp.where(t_in_seq == 0, 0.0, _rms(halo_ref[...], g))
    buf_sc[:POOL_HALO, :] = halo
    buf_sc[POOL_HALO:, :] = xn
    t = t_in_seq * tile + lax.broadcasted_iota(jnp.int32, (tile, 1), 0)
    for gi, win in enumerate(POOL_WINDOWS):
        cols = slice(gi * POOL_GROUP, (gi + 1) * POOL_GROUP)
        xg = xn[:, cols]
        acc = buf_sc[:, cols]
        span = 1
        while span < win:
            acc = acc + pltpu.roll(acc, span, 0)
            span *= 2
        acc = acc[POOL_HALO:, :]
        cnt = jnp.minimum(t + 1, win).astype(F32)
        y = (acc / cnt - xg).astype(BF16)
        mix = jnp.dot(y, w_ref[gi], preferred_element_type=F32)
        o_ref[:, cols] = x[:, cols] + mix * ps_ref[:, cols]


def _pool_mix(x, g, w, ps, *, seq):
    t = x.shape[0]
    tiles_per_seq = seq // ROW_TILE
    halo_blocks = ROW_TILE // POOL_HALO
    row = lambda i: (i, 0)
    fixed = lambda i: (0, 0)
    return pl.pallas_call(
        functools.partial(_pool_mix_kernel, tiles_per_seq=tiles_per_seq),
        grid=(t // ROW_TILE,),
        in_specs=[
            pl.BlockSpec((ROW_TILE, D_MODEL), row),
            pl.BlockSpec((POOL_HALO, D_MODEL), lambda i: (jnp.maximum(i * halo_blocks - 1, 0), 0)),
            pl.BlockSpec((1, D_MODEL), fixed),
            pl.BlockSpec(w.shape, lambda i: (0, 0, 0)),
            pl.BlockSpec((1, D_MODEL), fixed),
        ],
        out_specs=pl.BlockSpec((ROW_TILE, D_MODEL), row),
        out_shape=jax.ShapeDtypeStruct((t, D_MODEL), F32),
        scratch_shapes=[pltpu.VMEM((ROW_TILE + POOL_HALO, D_MODEL), F32)],
        compiler_params=_params("parallel"),
        name="pool_mix",
    )(x, x, g, w, ps)


def kernel(x, positions, ln_mix, ln_mlp, w_dqkv, q_norm, kv_norm, w_uq, w_ukv, w_o,
           w_pool, pool_scale, w_up, w_down, final_norm):
    batch, seq, d = x.shape
    assert d == D_MODEL and seq % ATTN_TILE == 0 and seq % ROW_TILE == 0
    t = batch * seq
    h = x.reshape(t, d)

    w_dqkv_p = jnp.pad(w_dqkv[0], ((0, 0), (0, LANES - QK_ROPE))).astype(BF16)
    wq = w_uq[0].reshape(Q_LORA, N_HEADS, QK_HEAD)
    wq = jnp.concatenate(
        [wq[:, :, :QK_NOPE], jnp.pad(wq[:, :, QK_NOPE:], ((0, 0), (0, 0), (0, LANES - QK_ROPE)))],
        axis=-1).reshape(Q_LORA, N_HEADS * QK_PAD).astype(BF16)
    wkv = w_ukv[0].reshape(KV_LORA, N_HEADS, QK_NOPE + V_HEAD)
    wk = wkv[:, :, :QK_NOPE].reshape(KV_LORA, N_HEADS * QK_NOPE).astype(BF16)
    wvt = wkv[:, :, QK_NOPE:].reshape(KV_LORA, N_HEADS * V_HEAD).T.astype(BF16)
    wo = w_o[0].astype(BF16)
    wp = w_pool[0].astype(BF16)
    wu = w_up.astype(BF16)
    wd = w_down.astype(BF16)

    inv_freq = 1.0 / (ROPE_THETA ** (jnp.arange(0, QK_ROPE, 2, dtype=F32) / QK_ROPE))
    invf = jnp.concatenate([inv_freq, inv_freq, jnp.zeros((LANES - QK_ROPE,), F32)]).reshape(1, LANES)
    pos = jnp.broadcast_to(positions.reshape(t, 1), (t, LANES))
    scale = math.log2(math.e) / math.sqrt(QK_HEAD)

    row = lambda v: v.reshape(1, -1)

    cq, ckv, kr = _attn_down(h, row(ln_mix[0]), w_dqkv_p, row(q_norm[0]), row(kv_norm[0]), pos, invf)
    q, k, vt = _attn_up(cq, ckv, kr, pos, invf, wq, wk, wvt, batch=batch, seq=seq, scale=scale)
    o = _flash(q, k, vt, batch=batch, seq=seq)
    h = _out_proj(h, o, wo, seq=seq)
    h = _mlp(h, row(ln_mlp[0]), wu, wd, row(final_norm), layer=0, final_norm=False)

    h = _pool_mix(h, row(ln_mix[1]), wp, row(pool_scale[0]), seq=seq)
    h = _mlp(h, row(ln_mlp[1]), wu, wd, row(final_norm), layer=1, final_norm=True)
    return h.reshape(batch, seq, d)
```

```python
import functools
import math

import jax
import jax.numpy as jnp
from jax import lax
from jax.experimental import pallas as pl
from jax.experimental.pallas import tpu as pltpu

D_MODEL = 2048
N_HEADS = 16
Q_LORA = 512
KV_LORA = 512
QK_NOPE = 128
QK_ROPE = 64
V_HEAD = 128
QK_HEAD = QK_NOPE + QK_ROPE
ROPE_THETA = 10000.0
POOL_WINDOWS = (2, 4, 8, 16)
POOL_GROUP = D_MODEL // len(POOL_WINDOWS)
D_FF = 4 * D_MODEL
EPS = 1e-6

LANES = 128
QK_PAD = 2 * LANES
V_ROWS = V_HEAD + 16
POOL_HALO = 16
VMEM_LIMIT_BYTES = 56 * 1024 * 1024

F32 = jnp.float32
BF16 = jnp.bfloat16

ROW_TILE = 512
DOWN_SUB_ROWS = 128
ATTN_TILE = 512
FLASH_UNROLL = 14
MLP_ROW_TILE = 1024
MLP_FF_TILE = 512


def _params(*semantics):
    return pltpu.CompilerParams(dimension_semantics=semantics,
                                vmem_limit_bytes=VMEM_LIMIT_BYTES)


def _rms(x, g):
    ms = jnp.mean(x * x, axis=-1, keepdims=True)
    return x * lax.rsqrt(ms + EPS) * g


def _rope_tables(pos, inv_freq):
    ang = pos.astype(F32) * inv_freq
    cos = jnp.cos(ang)
    sin = jnp.sin(ang)
    lane = lax.broadcasted_iota(jnp.int32, ang.shape, 1)
    first_half = (lane & (QK_ROPE // 2)) == 0
    return cos, jnp.where(first_half, -sin, 0.0), jnp.where(first_half, 0.0, sin)


def _rope(x, cos, sin_lo, sin_hi):
    half = QK_ROPE // 2
    return (x * cos + pltpu.roll(x, LANES - half, 1) * sin_lo
            + pltpu.roll(x, half, 1) * sin_hi)


def _attn_down_kernel(x_ref, g_ref, w_ref, qn_ref, kvn_ref, pos_ref, invf_ref,
                      cq_ref, ckv_ref, kr_ref):
    for r in range(ROW_TILE // DOWN_SUB_ROWS):
        rs = slice(r * DOWN_SUB_ROWS, (r + 1) * DOWN_SUB_ROWS)
        xn = _rms(x_ref[rs, :], g_ref[...]).astype(BF16)
        c = jnp.dot(xn, w_ref[...], preferred_element_type=F32)
        cq_ref[rs, :] = _rms(c[:, :Q_LORA], qn_ref[...]).astype(BF16)
        ckv_ref[rs, :] = _rms(c[:, Q_LORA:Q_LORA + KV_LORA], kvn_ref[...]).astype(BF16)
        cos, sin_lo, sin_hi = _rope_tables(pos_ref[rs, :], invf_ref[...])
        kr_ref[rs, :] = _rope(c[:, Q_LORA + KV_LORA:], cos, sin_lo, sin_hi).astype(BF16)


def _attn_down(x, g, w, qn, kvn, pos, invf):
    t = x.shape[0]
    n = w.shape[1]
    row = lambda i: (i, 0)
    fixed = lambda i: (0, 0)
    return pl.pallas_call(
        _attn_down_kernel,
        grid=(t // ROW_TILE,),
        in_specs=[
            pl.BlockSpec((ROW_TILE, D_MODEL), row),
            pl.BlockSpec((1, D_MODEL), fixed),
            pl.BlockSpec((D_MODEL, n), fixed),
            pl.BlockSpec((1, Q_LORA), fixed),
            pl.BlockSpec((1, KV_LORA), fixed),
            pl.BlockSpec((ROW_TILE, LANES), row),
            pl.BlockSpec((1, LANES), fixed),
        ],
        out_specs=[
            pl.BlockSpec((ROW_TILE, Q_LORA), row),
            pl.BlockSpec((ROW_TILE, KV_LORA), row),
            pl.BlockSpec((ROW_TILE, LANES), row),
        ],
        out_shape=[
            jax.ShapeDtypeStruct((t, Q_LORA), BF16),
            jax.ShapeDtypeStruct((t, KV_LORA), BF16),
            jax.ShapeDtypeStruct((t, LANES), BF16),
        ],
        compiler_params=_params("parallel"),
        name="attn_down",
    )(x, g, w, qn, kvn, pos, invf)


def _attn_up_kernel(cq_ref, ckv_ref, kr_ref, pos_ref, invf_ref, wq_ref, wk_ref, wvt_ref,
                    q_ref, k_ref, vt_ref, *, scale):
    cos, sin_lo, sin_hi = _rope_tables(pos_ref[...], invf_ref[...])
    cq = cq_ref[...]
    ckv = ckv_ref[...]
    kr = kr_ref[...]
    for h in range(N_HEADS):
        lo = h * QK_PAD
        qh = jnp.dot(cq, wq_ref[:, lo:lo + QK_PAD], preferred_element_type=F32)
        q_ref[h, :, :QK_NOPE] = (qh[:, :QK_NOPE] * scale).astype(BF16)
        q_ref[h, :, QK_NOPE:] = (_rope(qh[:, QK_NOPE:], cos, sin_lo, sin_hi) * scale).astype(BF16)
        k_ref[h, :, QK_NOPE:] = kr
    for p in range(N_HEADS // 2):
        kk = jnp.dot(ckv, wk_ref[:, p * QK_PAD:(p + 1) * QK_PAD], preferred_element_type=F32)
        k_ref[2 * p, :, :QK_NOPE] = kk[:, :QK_NOPE].astype(BF16)
        k_ref[2 * p + 1, :, :QK_NOPE] = kk[:, QK_NOPE:].astype(BF16)
    pad_rows = V_ROWS - V_HEAD
    ones_row = (lax.broadcasted_iota(jnp.int32, (pad_rows, ATTN_TILE), 0) == 0).astype(BF16)
    group = 4
    for c in range(N_HEADS // group):
        vt = lax.dot_general(wvt_ref[c * group * V_HEAD:(c + 1) * group * V_HEAD, :], ckv,
                             (((1,), (1,)), ((), ())), preferred_element_type=F32)
        for hh in range(group):
            vt_ref[c * group + hh, :V_HEAD, :] = vt[hh * V_HEAD:(hh + 1) * V_HEAD, :].astype(BF16)
            vt_ref[c * group + hh, V_HEAD:, :] = ones_row


def _attn_up(cq, ckv, kr, pos, invf, wq, wk, wvt, *, batch, seq, scale):
    t = cq.shape[0]
    tiles_per_seq = seq // ATTN_TILE
    row = lambda i: (i, 0)
    fixed = lambda i: (0, 0)
    head_major = lambda i: (i // tiles_per_seq, 0, i % tiles_per_seq, 0)
    return pl.pallas_call(
        functools.partial(_attn_up_kernel, scale=scale),
        grid=(t // ATTN_TILE,),
        in_specs=[
            pl.BlockSpec((ATTN_TILE, Q_LORA), row),
            pl.BlockSpec((ATTN_TILE, KV_LORA), row),
            pl.BlockSpec((ATTN_TILE, LANES), row),
            pl.BlockSpec((ATTN_TILE, LANES), row),
            pl.BlockSpec((1, LANES), fixed),
            pl.BlockSpec(wq.shape, fixed),
            pl.BlockSpec(wk.shape, fixed),
            pl.BlockSpec(wvt.shape, fixed),
        ],
        out_specs=[
            pl.BlockSpec((None, N_HEADS, ATTN_TILE, QK_PAD), head_major),
            pl.BlockSpec((None, N_HEADS, ATTN_TILE, QK_PAD), head_major),
            pl.BlockSpec((None, None, N_HEADS, V_ROWS, ATTN_TILE),
                         lambda i: (i // tiles_per_seq, i % tiles_per_seq, 0, 0, 0)),
        ],
        out_shape=[
            jax.ShapeDtypeStruct((batch, N_HEADS, seq, QK_PAD), BF16),
            jax.ShapeDtypeStruct((batch, N_HEADS, seq, QK_PAD), BF16),
            jax.ShapeDtypeStruct((batch, tiles_per_seq, N_HEADS, V_ROWS, ATTN_TILE), BF16),
        ],
        compiler_params=_params("parallel"),
        name="attn_up",
    )(cq, ckv, kr, pos, invf, wq, wk, wvt)


def _flash_kernel(q_ref, k_ref, vt_ref, o_ref, s_sc, cmax_sc, m_sc, acc_sc, *, n_tiles):
    tile = ATTN_TILE

    def rows(idx):
        if isinstance(idx, int):
            return pl.ds(idx * tile, tile)
        return pl.ds(pl.multiple_of(idx * tile, tile), tile)

    def produce(i, j, slot, diagonal):
        s = lax.dot_general(k_ref[rows(j), :], q_ref[rows(i), :],
                            (((1,), (1,)), ((), ())), preferred_element_type=F32)
        if diagonal:
            kpos = lax.broadcasted_iota(jnp.int32, s.shape, 0)
            qpos = lax.broadcasted_iota(jnp.int32, s.shape, 1)
            s = jnp.where(kpos <= qpos, s, -jnp.inf)
        s_sc[slot] = s
        cmax_sc[slot] = jnp.max(s, axis=0, keepdims=True)

    def consume(i, j, slot):
        m_prev = m_sc[i]
        m_new = jnp.maximum(m_prev, cmax_sc[slot])
        alpha = jnp.exp2(m_prev - m_new)
        p = jnp.exp2(s_sc[slot] - m_new).astype(BF16)
        m_sc[i] = m_new
        acc_sc[i] = alpha * acc_sc[i] + jnp.dot(vt_ref[j], p, preferred_element_type=F32)

    m_sc[...] = jnp.full(m_sc.shape, -jnp.inf, F32)
    acc_sc[...] = jnp.zeros(acc_sc.shape, F32)

    assert n_tiles % 2 == 0
    produce(0, 0, 0, True)
    for i in range(n_tiles):
        if i + 1 < n_tiles:
            produce(i + 1, i + 1, (i + 1) % 2, True)
        else:
            produce(1, 0, (i + 1) % 2, False)
        consume(i, i, i % 2)

    def succ(i, j):
        wrap = j + 1 >= i
        return jnp.where(wrap, i + 1, i), jnp.where(wrap, 0, j + 1)

    n_off = n_tiles * (n_tiles - 1) // 2
    assert n_off % FLASH_UNROLL == 0 and FLASH_UNROLL % 2 == 0

    def body(_, cur):
        for u in range(FLASH_UNROLL):
            nxt = succ(*cur)
            produce(jnp.minimum(nxt[0], n_tiles - 1), jnp.minimum(nxt[1], n_tiles - 2),
                    1 - u % 2, False)
            consume(*cur, u % 2)
            cur = nxt
        return cur

    lax.fori_loop(0, n_off // FLASH_UNROLL, body, (jnp.int32(1), jnp.int32(0)))

    for i in range(n_tiles):
        acc = acc_sc[i]
        o = acc[:V_HEAD, :] * (1.0 / acc[V_HEAD:V_HEAD + 1, :])
        o_ref[i * tile:(i + 1) * tile, :] = o.T.astype(BF16)


def _flash(q, k, vt, *, batch, seq):
    n_tiles = seq // ATTN_TILE
    return pl.pallas_call(
        functools.partial(_flash_kernel, n_tiles=n_tiles),
        grid=(batch, N_HEADS),
        in_specs=[
            pl.BlockSpec((None, None, seq, QK_PAD), lambda b, h: (b, h, 0, 0)),
            pl.BlockSpec((None, None, seq, QK_PAD), lambda b, h: (b, h, 0, 0)),
            pl.BlockSpec((None, n_tiles, None, V_ROWS, ATTN_TILE), lambda b, h: (b, 0, h, 0, 0)),
        ],
        out_specs=pl.BlockSpec((None, None, seq, V_HEAD), lambda b, h: (b, h, 0, 0)),
        out_shape=jax.ShapeDtypeStruct((batch, N_HEADS, seq, V_HEAD), BF16),
        scratch_shapes=[
            pltpu.VMEM((2, ATTN_TILE, ATTN_TILE), F32),
            pltpu.VMEM((2, 1, ATTN_TILE), F32),
            pltpu.VMEM((n_tiles, 1, ATTN_TILE), F32),
            pltpu.VMEM((n_tiles, V_ROWS, ATTN_TILE), F32),
        ],
        compiler_params=_params("parallel", "parallel"),
        name="flash_attn",
    )(q, k, vt)


def _out_proj_kernel(x_ref, o_ref, w_ref, h_ref):
    o = jnp.concatenate([o_ref[h] for h in range(N_HEADS)], axis=-1)
    h_ref[...] = x_ref[...] + jnp.dot(o, w_ref[...], preferred_element_type=F32)


def _out_proj(x, o, w, *, seq):
    t = x.shape[0]
    tiles_per_seq = seq // ROW_TILE
    row = lambda i: (i, 0)
    return pl.pallas_call(
        _out_proj_kernel,
        grid=(t // ROW_TILE,),
        in_specs=[
            pl.BlockSpec((ROW_TILE, D_MODEL), row),
            pl.BlockSpec((None, N_HEADS, ROW_TILE, V_HEAD),
                         lambda i: (i // tiles_per_seq, 0, i % tiles_per_seq, 0)),
            pl.BlockSpec(w.shape, lambda i: (0, 0)),
        ],
        out_specs=pl.BlockSpec((ROW_TILE, D_MODEL), row),
        out_shape=jax.ShapeDtypeStruct((t, D_MODEL), F32),
        compiler_params=_params("parallel"),
        name="out_proj",
    )(x, o, w)


def _mlp_kernel(x_ref, g_ref, wu_ref, wd_ref, fg_ref, o_ref, xn_sc, *, final_norm):
    j = pl.program_id(1)

    @pl.when(j == 0)
    def _():
        x = x_ref[...]
        xn_sc[...] = _rms(x, g_ref[...]).astype(BF16)
        o_ref[...] = x

    u = jnp.dot(xn_sc[...], wu_ref[...], preferred_element_type=F32)
    u = jnp.maximum(u, 0.0)
    o_ref[...] += jnp.dot((u * u).astype(BF16), wd_ref[...], preferred_element_type=F32)

    if final_norm:
        @pl.when(j == pl.num_programs(1) - 1)
        def _():
            o_ref[...] = _rms(o_ref[...], fg_ref[...])


def _mlp(x, g, wu, wd, fg, *, layer, final_norm):
    t = x.shape[0]
    row = lambda i, j: (i, 0)
    fixed = lambda i, j: (0, 0)
    return pl.pallas_call(
        functools.partial(_mlp_kernel, final_norm=final_norm),
        grid=(t // MLP_ROW_TILE, D_FF // MLP_FF_TILE),
        in_specs=[
            pl.BlockSpec((MLP_ROW_TILE, D_MODEL), row),
            pl.BlockSpec((1, D_MODEL), fixed),
            pl.BlockSpec((None, D_MODEL, MLP_FF_TILE), lambda i, j: (layer, 0, j)),
            pl.BlockSpec((None, MLP_FF_TILE, D_MODEL), lambda i, j: (layer, j, 0)),
            pl.BlockSpec((1, D_MODEL), fixed),
        ],
        out_specs=pl.BlockSpec((MLP_ROW_TILE, D_MODEL), row),
        out_shape=jax.ShapeDtypeStruct((t, D_MODEL), F32),
        scratch_shapes=[pltpu.VMEM((MLP_ROW_TILE, D_MODEL), BF16)],
        compiler_params=_params("parallel", "arbitrary"),
        name="mlp_final" if final_norm else "mlp",
    )(x, g, wu, wd, fg)


def _pool_mix_kernel(x_ref, halo_ref, g_ref, w_ref, ps_ref, o_ref, buf_sc, *, tiles_per_seq):
    tile = ROW_TILE
    i = pl.program_id(0)
    x = x_ref[...]
    g = g_ref[...]
    xn = _rms(x, g)
    t_in_seq = i % tiles_per_seq
    halo = jnp.where(t_in_seq == 0, 0.0, _rms(halo_ref[...], g))
    buf_sc[:POOL_HALO, :] = halo
    buf_sc[POOL_HALO:, :] = xn
    t = t_in_seq * tile + lax.broadcasted_iota(jnp.int32, (tile, 1), 0)
    for gi, win in enumerate(POOL_WINDOWS):
        cols = slice(gi * POOL_GROUP, (gi + 1) * POOL_GROUP)
        xg = xn[:, cols]
        acc = buf_sc[:, cols]
        span = 1
        while span < win:
            acc = acc + pltpu.roll(acc, span, 0)
            span *= 2
        acc = acc[POOL_HALO:, :]
        cnt = jnp.minimum(t + 1, win).astype(F32)
        y = (acc / cnt - xg).astype(BF16)
        mix = jnp.dot(y, w_ref[gi], preferred_element_type=F32)
        o_ref[:, cols] = x[:, cols] + mix * ps_ref[:, cols]


def _pool_mix(x, g, w, ps, *, seq):
    t = x.shape[0]
    tiles_per_seq = seq // ROW_TILE
    halo_blocks = ROW_TILE // POOL_HALO
    row = lambda i: (i, 0)
    fixed = lambda i: (0, 0)
    return pl.pallas_call(
        functools.partial(_pool_mix_kernel, tiles_per_seq=tiles_per_seq),
        grid=(t // ROW_TILE,),
        in_specs=[
            pl.BlockSpec((ROW_TILE, D_MODEL), row),
            pl.BlockSpec((POOL_HALO, D_MODEL), lambda i: (jnp.maximum(i * halo_blocks - 1, 0), 0)),
            pl.BlockSpec((1, D_MODEL), fixed),
            pl.BlockSpec(w.shape, lambda i: (0, 0, 0)),
            pl.BlockSpec((1, D_MODEL), fixed),
        ],
        out_specs=pl.BlockSpec((ROW_TILE, D_MODEL), row),
        out_shape=jax.ShapeDtypeStruct((t, D_MODEL), F32),
        scratch_shapes=[pltpu.VMEM((ROW_TILE + POOL_HALO, D_MODEL), F32)],
        compiler_params=_params("parallel"),
        name="pool_mix",
    )(x, x, g, w, ps)


def kernel(x, positions, ln_mix, ln_mlp, w_dqkv, q_norm, kv_norm, w_uq, w_ukv, w_o,
           w_pool, pool_scale, w_up, w_down, final_norm):
    batch, seq, d = x.shape
    assert d == D_MODEL and seq % ATTN_TILE == 0 and seq % ROW_TILE == 0
    t = batch * seq
    h = x.reshape(t, d)

    w_dqkv_p = jnp.pad(w_dqkv[0], ((0, 0), (0, LANES - QK_ROPE))).astype(BF16)
    wq = w_uq[0].reshape(Q_LORA, N_HEADS, QK_HEAD)
    wq = jnp.concatenate(
        [wq[:, :, :QK_NOPE], jnp.pad(wq[:, :, QK_NOPE:], ((0, 0), (0, 0), (0, LANES - QK_ROPE)))],
        axis=-1).reshape(Q_LORA, N_HEADS * QK_PAD).astype(BF16)
    wkv = w_ukv[0].reshape(KV_LORA, N_HEADS, QK_NOPE + V_HEAD)
    wk = wkv[:, :, :QK_NOPE].reshape(KV_LORA, N_HEADS * QK_NOPE).astype(BF16)
    wvt = wkv[:, :, QK_NOPE:].reshape(KV_LORA, N_HEADS * V_HEAD).T.astype(BF16)
    wo = w_o[0].astype(BF16)
    wp = w_pool[0].astype(BF16)
    wu = w_up.astype(BF16)
    wd = w_down.astype(BF16)

    inv_freq = 1.0 / (ROPE_THETA ** (jnp.arange(0, QK_ROPE, 2, dtype=F32) / QK_ROPE))
    invf = jnp.concatenate([inv_freq, inv_freq, jnp.zeros((LANES - QK_ROPE,), F32)]).reshape(1, LANES)
    pos = jnp.broadcast_to(positions.reshape(t, 1), (t, LANES))
    scale = math.log2(math.e) / math.sqrt(QK_HEAD)

    row = lambda v: v.reshape(1, -1)

    cq, ckv, kr = _attn_down(h, row(ln_mix[0]), w_dqkv_p, row(q_norm[0]), row(kv_norm[0]), pos, invf)
    q, k, vt = _attn_up(cq, ckv, kr, pos, invf, wq, wk, wvt, batch=batch, seq=seq, scale=scale)
    o = _flash(q, k, vt, batch=batch, seq=seq)
    h = _out_proj(h, o, wo, seq=seq)
    h = _mlp(h, row(ln_mlp[0]), wu, wd, row(final_norm), layer=0, final_norm=False)

    h = _pool_mix(h, row(ln_mix[1]), wp, row(pool_scale[0]), seq=seq)
    h = _mlp(h, row(ln_mlp[1]), wu, wd, row(final_norm), layer=1, final_norm=True)
    return h.reshape(batch, seq, d)
```

```python
import functools
import math

import jax
import jax.numpy as jnp
from jax import lax
from jax.experimental import pallas as pl
from jax.experimental.pallas import tpu as pltpu

D_MODEL = 2048
N_HEADS = 16
Q_LORA = 512
KV_LORA = 512
QK_NOPE = 128
QK_ROPE = 64
V_HEAD = 128
QK_HEAD = QK_NOPE + QK_ROPE
ROPE_THETA = 10000.0
POOL_WINDOWS = (2, 4, 8, 16)
POOL_GROUP = D_MODEL // len(POOL_WINDOWS)
D_FF = 4 * D_MODEL
EPS = 1e-6

LANES = 128
QK_PAD = 2 * LANES
V_ROWS = V_HEAD + 16
POOL_HALO = 16
VMEM_LIMIT_BYTES = 56 * 1024 * 1024

F32 = jnp.float32
BF16 = jnp.bfloat16

ROW_TILE = 512
DOWN_SUB_ROWS = 128
ATTN_TILE = 512
FLASH_UNROLL = 14
MLP_ROW_TILE = 512
MLP_FF_TILE = 1024


def _params(*semantics):
    return pltpu.CompilerParams(dimension_semantics=semantics,
                                vmem_limit_bytes=VMEM_LIMIT_BYTES)


def _rms(x, g):
    ms = jnp.mean(x * x, axis=-1, keepdims=True)
    return x * lax.rsqrt(ms + EPS) * g


def _rope_tables(pos, inv_freq):
    ang = pos.astype(F32) * inv_freq
    cos = jnp.cos(ang)
    sin = jnp.sin(ang)
    lane = lax.broadcasted_iota(jnp.int32, ang.shape, 1)
    first_half = (lane & (QK_ROPE // 2)) == 0
    return cos, jnp.where(first_half, -sin, 0.0), jnp.where(first_half, 0.0, sin)


def _rope(x, cos, sin_lo, sin_hi):
    half = QK_ROPE // 2
    return (x * cos + pltpu.roll(x, LANES - half, 1) * sin_lo
            + pltpu.roll(x, half, 1) * sin_hi)


def _attn_down_kernel(x_ref, g_ref, w_ref, qn_ref, kvn_ref, pos_ref, invf_ref,
                      cq_ref, ckv_ref, kr_ref):
    for r in range(ROW_TILE // DOWN_SUB_ROWS):
        rs = slice(r * DOWN_SUB_ROWS, (r + 1) * DOWN_SUB_ROWS)
        xn = _rms(x_ref[rs, :], g_ref[...]).astype(BF16)
        c = jnp.dot(xn, w_ref[...], preferred_element_type=F32)
        cq_ref[rs, :] = _rms(c[:, :Q_LORA], qn_ref[...]).astype(BF16)
        ckv_ref[rs, :] = _rms(c[:, Q_LORA:Q_LORA + KV_LORA], kvn_ref[...]).astype(BF16)
        cos, sin_lo, sin_hi = _rope_tables(pos_ref[rs, :], invf_ref[...])
        kr_ref[rs, :] = _rope(c[:, Q_LORA + KV_LORA:], cos, sin_lo, sin_hi).astype(BF16)


def _attn_down(x, g, w, qn, kvn, pos, invf):
    t = x.shape[0]
    n = w.shape[1]
    row = lambda i: (i, 0)
    fixed = lambda i: (0, 0)
    return pl.pallas_call(
        _attn_down_kernel,
        grid=(t // ROW_TILE,),
        in_specs=[
            pl.BlockSpec((ROW_TILE, D_MODEL), row),
            pl.BlockSpec((1, D_MODEL), fixed),
            pl.BlockSpec((D_MODEL, n), fixed),
            pl.BlockSpec((1, Q_LORA), fixed),
            pl.BlockSpec((1, KV_LORA), fixed),
            pl.BlockSpec((ROW_TILE, LANES), row),
            pl.BlockSpec((1, LANES), fixed),
        ],
        out_specs=[
            pl.BlockSpec((ROW_TILE, Q_LORA), row),
            pl.BlockSpec((ROW_TILE, KV_LORA), row),
            pl.BlockSpec((ROW_TILE, LANES), row),
        ],
        out_shape=[
            jax.ShapeDtypeStruct((t, Q_LORA), BF16),
            jax.ShapeDtypeStruct((t, KV_LORA), BF16),
            jax.ShapeDtypeStruct((t, LANES), BF16),
        ],
        compiler_params=_params("parallel"),
        name="attn_down",
    )(x, g, w, qn, kvn, pos, invf)


def _attn_up_kernel(cq_ref, ckv_ref, kr_ref, pos_ref, invf_ref, wqt_ref, wk_ref, wvt_ref,
                    qt_ref, k_ref, vt_ref, *, scale):
    cq = cq_ref[...]
    ckv = ckv_ref[...]
    kr = kr_ref[...]
    half = QK_ROPE // 2
    ang = invf_ref[...] * pos_ref[...].astype(F32)
    cos = jnp.cos(ang)
    sin = jnp.sin(ang)
    zeros = jnp.zeros((QK_PAD - QK_HEAD, ATTN_TILE), BF16)
    group = 4
    for c in range(N_HEADS // group):
        qt = lax.dot_general(wqt_ref[c * group * QK_HEAD:(c + 1) * group * QK_HEAD, :], cq,
                             (((1,), (1,)), ((), ())), preferred_element_type=F32)
        for hh in range(group):
            h = c * group + hh
            qh = qt[hh * QK_HEAD:(hh + 1) * QK_HEAD, :]
            x1 = qh[QK_NOPE:QK_NOPE + half, :]
            x2 = qh[QK_NOPE + half:, :]
            qt_ref[h, :QK_NOPE, :] = (qh[:QK_NOPE, :] * scale).astype(BF16)
            qt_ref[h, QK_NOPE:QK_NOPE + half, :] = ((x1 * cos - x2 * sin) * scale).astype(BF16)
            qt_ref[h, QK_NOPE + half:QK_HEAD, :] = ((x2 * cos + x1 * sin) * scale).astype(BF16)
            qt_ref[h, QK_HEAD:, :] = zeros
            k_ref[h, :, QK_NOPE:] = kr
    for p in range(N_HEADS // 2):
        kk = jnp.dot(ckv, wk_ref[:, p * QK_PAD:(p + 1) * QK_PAD], preferred_element_type=F32)
        k_ref[2 * p, :, :QK_NOPE] = kk[:, :QK_NOPE].astype(BF16)
        k_ref[2 * p + 1, :, :QK_NOPE] = kk[:, QK_NOPE:].astype(BF16)
    pad_rows = V_ROWS - V_HEAD
    ones_row = (lax.broadcasted_iota(jnp.int32, (pad_rows, ATTN_TILE), 0) == 0).astype(BF16)
    group = 4
    for c in range(N_HEADS // group):
        vt = lax.dot_general(wvt_ref[c * group * V_HEAD:(c + 1) * group * V_HEAD, :], ckv,
                             (((1,), (1,)), ((), ())), preferred_element_type=F32)
        for hh in range(group):
            vt_ref[c * group + hh, :V_HEAD, :] = vt[hh * V_HEAD:(hh + 1) * V_HEAD, :].astype(BF16)
            vt_ref[c * group + hh, V_HEAD:, :] = ones_row


def _attn_up(cq, ckv, kr, pos_rows, invf_col, wqt, wk, wvt, *, batch, seq, scale):
    t = cq.shape[0]
    tiles_per_seq = seq // ATTN_TILE
    row = lambda i: (i, 0)
    fixed = lambda i: (0, 0)
    head_major = lambda i: (i // tiles_per_seq, 0, i % tiles_per_seq, 0)
    chunk_major = lambda i: (i // tiles_per_seq, i % tiles_per_seq, 0, 0, 0)
    return pl.pallas_call(
        functools.partial(_attn_up_kernel, scale=scale),
        grid=(t // ATTN_TILE,),
        in_specs=[
            pl.BlockSpec((ATTN_TILE, Q_LORA), row),
            pl.BlockSpec((ATTN_TILE, KV_LORA), row),
            pl.BlockSpec((ATTN_TILE, LANES), row),
            pl.BlockSpec((None, 1, ATTN_TILE), lambda i: (i, 0, 0)),
            pl.BlockSpec(invf_col.shape, fixed),
            pl.BlockSpec(wqt.shape, fixed),
            pl.BlockSpec(wk.shape, fixed),
            pl.BlockSpec(wvt.shape, fixed),
        ],
        out_specs=[
            pl.BlockSpec((None, None, N_HEADS, QK_PAD, ATTN_TILE), chunk_major),
            pl.BlockSpec((None, N_HEADS, ATTN_TILE, QK_PAD), head_major),
            pl.BlockSpec((None, None, N_HEADS, V_ROWS, ATTN_TILE), chunk_major),
        ],
        out_shape=[
            jax.ShapeDtypeStruct((batch, tiles_per_seq, N_HEADS, QK_PAD, ATTN_TILE), BF16),
            jax.ShapeDtypeStruct((batch, N_HEADS, seq, QK_PAD), BF16),
            jax.ShapeDtypeStruct((batch, tiles_per_seq, N_HEADS, V_ROWS, ATTN_TILE), BF16),
        ],
        compiler_params=_params("parallel"),
        name="attn_up",
    )(cq, ckv, kr, pos_rows, invf_col, wqt, wk, wvt)


def _flash_kernel(qt_ref, k_ref, vt_ref, o_ref, s_sc, cmax_sc, m_sc, acc_sc, *, n_tiles):
    tile = ATTN_TILE

    def rows(idx):
        if isinstance(idx, int):
            return pl.ds(idx * tile, tile)
        return pl.ds(pl.multiple_of(idx * tile, tile), tile)

    def produce(i, j, slot, diagonal):
        s = jnp.dot(k_ref[rows(j), :], qt_ref[i], preferred_element_type=F32)
        if diagonal:
            kpos = lax.broadcasted_iota(jnp.int32, s.shape, 0)
            qpos = lax.broadcasted_iota(jnp.int32, s.shape, 1)
            s = jnp.where(kpos <= qpos, s, -jnp.inf)
        s_sc[slot] = s
        cmax_sc[slot] = jnp.max(s, axis=0, keepdims=True)

    def consume(i, j, slot):
        m_prev = m_sc[i]
        m_new = jnp.maximum(m_prev, cmax_sc[slot])
        alpha = jnp.exp2(m_prev - m_new)
        p = jnp.exp2(s_sc[slot] - m_new).astype(BF16)
        m_sc[i] = m_new
        acc_sc[i] = alpha * acc_sc[i] + jnp.dot(vt_ref[j], p, preferred_element_type=F32)

    m_sc[...] = jnp.full(m_sc.shape, -jnp.inf, F32)
    acc_sc[...] = jnp.zeros(acc_sc.shape, F32)

    assert n_tiles % 2 == 0
    produce(0, 0, 0, True)
    for i in range(n_tiles):
        if i + 1 < n_tiles:
            produce(i + 1, i + 1, (i + 1) % 2, True)
        else:
            produce(1, 0, (i + 1) % 2, False)
        consume(i, i, i % 2)

    def succ(i, j):
        wrap = j + 1 >= i
        return jnp.where(wrap, i + 1, i), jnp.where(wrap, 0, j + 1)

    n_off = n_tiles * (n_tiles - 1) // 2
    assert n_off % FLASH_UNROLL == 0 and FLASH_UNROLL % 2 == 0

    def body(_, cur):
        for u in range(FLASH_UNROLL):
            nxt = succ(*cur)
            produce(jnp.minimum(nxt[0], n_tiles - 1), jnp.minimum(nxt[1], n_tiles - 2),
                    1 - u % 2, False)
            consume(*cur, u % 2)
            cur = nxt
        return cur

    lax.fori_loop(0, n_off // FLASH_UNROLL, body, (jnp.int32(1), jnp.int32(0)))

    for i in range(n_tiles):
        acc = acc_sc[i]
        o = acc[:V_HEAD, :] * (1.0 / acc[V_HEAD:V_HEAD + 1, :])
        o_ref[i * tile:(i + 1) * tile, :] = o.T.astype(BF16)


def _flash(q, k, vt, *, batch, seq):
    n_tiles = seq // ATTN_TILE
    return pl.pallas_call(
        functools.partial(_flash_kernel, n_tiles=n_tiles),
        grid=(batch, N_HEADS),
        in_specs=[
            pl.BlockSpec((None, n_tiles, None, QK_PAD, ATTN_TILE), lambda b, h: (b, 0, h, 0, 0)),
            pl.BlockSpec((None, None, seq, QK_PAD), lambda b, h: (b, h, 0, 0)),
            pl.BlockSpec((None, n_tiles, None, V_ROWS, ATTN_TILE), lambda b, h: (b, 0, h, 0, 0)),
        ],
        out_specs=pl.BlockSpec((None, None, seq, V_HEAD), lambda b, h: (b, h, 0, 0)),
        out_shape=jax.ShapeDtypeStruct((batch, N_HEADS, seq, V_HEAD), BF16),
        scratch_shapes=[
            pltpu.VMEM((2, ATTN_TILE, ATTN_TILE), F32),
            pltpu.VMEM((2, 1, ATTN_TILE), F32),
            pltpu.VMEM((n_tiles, 1, ATTN_TILE), F32),
            pltpu.VMEM((n_tiles, V_ROWS, ATTN_TILE), F32),
        ],
        compiler_params=_params("parallel", "parallel"),
        name="flash_attn",
    )(q, k, vt)


def _out_proj_kernel(x_ref, o_ref, w_ref, h_ref):
    o = jnp.concatenate([o_ref[h] for h in range(N_HEADS)], axis=-1)
    h_ref[...] = x_ref[...] + jnp.dot(o, w_ref[...], preferred_element_type=F32)


def _out_proj(x, o, w, *, seq):
    t = x.shape[0]
    tiles_per_seq = seq // ROW_TILE
    row = lambda i: (i, 0)
    return pl.pallas_call(
        _out_proj_kernel,
        grid=(t // ROW_TILE,),
        in_specs=[
            pl.BlockSpec((ROW_TILE, D_MODEL), row),
            pl.BlockSpec((None, N_HEADS, ROW_TILE, V_HEAD),
                         lambda i: (i // tiles_per_seq, 0, i % tiles_per_seq, 0)),
            pl.BlockSpec(w.shape, lambda i: (0, 0)),
        ],
        out_specs=pl.BlockSpec((ROW_TILE, D_MODEL), row),
        out_shape=jax.ShapeDtypeStruct((t, D_MODEL), F32),
        compiler_params=_params("parallel"),
        name="out_proj",
    )(x, o, w)


def _mlp_kernel(x_ref, g_ref, wu_ref, wd_ref, fg_ref, o_ref, xn_sc, *, final_norm):
    j = pl.program_id(1)

    @pl.when(j == 0)
    def _():
        x = x_ref[...]
        xn_sc[...] = _rms(x, g_ref[...]).astype(BF16)
        o_ref[...] = x

    u = jnp.dot(xn_sc[...], wu_ref[...], preferred_element_type=F32)
    u = jnp.maximum(u, 0.0)
    o_ref[...] += jnp.dot((u * u).astype(BF16), wd_ref[...], preferred_element_type=F32)

    if final_norm:
        @pl.when(j == pl.num_programs(1) - 1)
        def _():
            o_ref[...] = _rms(o_ref[...], fg_ref[...])


def _mlp(x, g, wu, wd, fg, *, layer, final_norm):
    t = x.shape[0]
    row = lambda i, j: (i, 0)
    fixed = lambda i, j: (0, 0)
    return pl.pallas_call(
        functools.partial(_mlp_kernel, final_norm=final_norm),
        grid=(t // MLP_ROW_TILE, D_FF // MLP_FF_TILE),
        in_specs=[
            pl.BlockSpec((MLP_ROW_TILE, D_MODEL), row),
            pl.BlockSpec((1, D_MODEL), fixed),
            pl.BlockSpec((None, D_MODEL, MLP_FF_TILE), lambda i, j: (layer, 0, j)),
            pl.BlockSpec((None, MLP_FF_TILE, D_MODEL), lambda i, j: (layer, j, 0)),
            pl.BlockSpec((1, D_MODEL), fixed),
        ],
        out_specs=pl.BlockSpec((MLP_ROW_TILE, D_MODEL), row),
        out_shape=jax.ShapeDtypeStruct((t, D_MODEL), F32),
        scratch_shapes=[pltpu.VMEM((MLP_ROW_TILE, D_MODEL), BF16)],
        compiler_params=_params("parallel", "arbitrary"),
        name="mlp_final" if final_norm else "mlp",
    )(x, g, wu, wd, fg)


def _pool_mix_kernel(x_ref, halo_ref, g_ref, w_ref, ps_ref, o_ref, buf_sc, *, tiles_per_seq):
    tile = ROW_TILE
    i = pl.program_id(0)
    x = x_ref[...]
    g = g_ref[...]
    xn = _rms(x, g)
    t_in_seq = i % tiles_per_seq
    halo = jnp.where(t_in_seq == 0, 0.0, _rms(halo_ref[...], g))
    buf_sc[:POOL_HALO, :] = halo
    buf_sc[POOL_HALO:, :] = xn
    t = t_in_seq * tile + lax.broadcasted_iota(jnp.int32, (tile, 1), 0)
    for gi, win in enumerate(POOL_WINDOWS):
        cols = slice(gi * POOL_GROUP, (gi + 1) * POOL_GROUP)
        xg = xn[:, cols]
        acc = buf_sc[:, cols]
        span = 1
        while span < win:
            acc = acc + pltpu.roll(acc, span, 0)
            span *= 2
        acc = acc[POOL_HALO:, :]
        cnt = jnp.minimum(t + 1, win).astype(F32)
        y = (acc / cnt - xg).astype(BF16)
        mix = jnp.dot(y, w_ref[gi], preferred_element_type=F32)
        o_ref[:, cols] = x[:, cols] + mix * ps_ref[:, cols]


def _pool_mix(x, g, w, ps, *, seq):
    t = x.shape[0]
    tiles_per_seq = seq // ROW_TILE
    halo_blocks = ROW_TILE // POOL_HALO
    row = lambda i: (i, 0)
    fixed = lambda i: (0, 0)
    return pl.pallas_call(
        functools.partial(_pool_mix_kernel, tiles_per_seq=tiles_per_seq),
        grid=(t // ROW_TILE,),
        in_specs=[
            pl.BlockSpec((ROW_TILE, D_MODEL), row),
            pl.BlockSpec((POOL_HALO, D_MODEL), lambda i: (jnp.maximum(i * halo_blocks - 1, 0), 0)),
            pl.BlockSpec((1, D_MODEL), fixed),
            pl.BlockSpec(w.shape, lambda i: (0, 0, 0)),
            pl.BlockSpec((1, D_MODEL), fixed),
        ],
        out_specs=pl.BlockSpec((ROW_TILE, D_MODEL), row),
        out_shape=jax.ShapeDtypeStruct((t, D_MODEL), F32),
        scratch_shapes=[pltpu.VMEM((ROW_TILE + POOL_HALO, D_MODEL), F32)],
        compiler_params=_params("parallel"),
        name="pool_mix",
    )(x, x, g, w, ps)


def kernel(x, positions, ln_mix, ln_mlp, w_dqkv, q_norm, kv_norm, w_uq, w_ukv, w_o,
           w_pool, pool_scale, w_up, w_down, final_norm):
    batch, seq, d = x.shape
    assert d == D_MODEL and seq % ATTN_TILE == 0 and seq % ROW_TILE == 0
    t = batch * seq
    h = x.reshape(t, d)

    w_dqkv_p = jnp.pad(w_dqkv[0], ((0, 0), (0, LANES - QK_ROPE))).astype(BF16)
    wqt = w_uq[0].T.astype(BF16)
    wkv = w_ukv[0].reshape(KV_LORA, N_HEADS, QK_NOPE + V_HEAD)
    wk = wkv[:, :, :QK_NOPE].reshape(KV_LORA, N_HEADS * QK_NOPE).astype(BF16)
    wvt = wkv[:, :, QK_NOPE:].reshape(KV_LORA, N_HEADS * V_HEAD).T.astype(BF16)
    wo = w_o[0].astype(BF16)
    wp = w_pool[0].astype(BF16)
    wu = w_up.astype(BF16)
    wd = w_down.astype(BF16)

    inv_freq = 1.0 / (ROPE_THETA ** (jnp.arange(0, QK_ROPE, 2, dtype=F32) / QK_ROPE))
    invf = jnp.concatenate([inv_freq, inv_freq, jnp.zeros((LANES - QK_ROPE,), F32)]).reshape(1, LANES)
    pos = jnp.broadcast_to(positions.reshape(t, 1), (t, LANES))
    pos_rows = positions.reshape(t // ATTN_TILE, 1, ATTN_TILE)
    invf_col = inv_freq.reshape(QK_ROPE // 2, 1)
    scale = math.log2(math.e) / math.sqrt(QK_HEAD)

    row = lambda v: v.reshape(1, -1)

    cq, ckv, kr = _attn_down(h, row(ln_mix[0]), w_dqkv_p, row(q_norm[0]), row(kv_norm[0]), pos, invf)
    qt, k, vt = _attn_up(cq, ckv, kr, pos_rows, invf_col, wqt, wk, wvt,
                         batch=batch, seq=seq, scale=scale)
    o = _flash(qt, k, vt, batch=batch, seq=seq)
    h = _out_proj(h, o, wo, seq=seq)
    h = _mlp(h, row(ln_mlp[0]), wu, wd, row(final_norm), layer=0, final_norm=False)

    h = _pool_mix(h, row(ln_mix[1]), wp, row(pool_scale[0]), seq=seq)
    h = _mlp(h, row(ln_mlp[1]), wu, wd, row(final_norm), layer=1, final_norm=True)
    return h.reshape(batch, seq, d)
```

```python
import functools
import math

import jax
import jax.numpy as jnp
from jax import lax
from jax.experimental import pallas as pl
from jax.experimental.pallas import tpu as pltpu

D_MODEL = 2048
N_HEADS = 16
Q_LORA = 512
KV_LORA = 512
QK_NOPE = 128
QK_ROPE = 64
V_HEAD = 128
QK_HEAD = QK_NOPE + QK_ROPE
ROPE_THETA = 10000.0
POOL_WINDOWS = (2, 4, 8, 16)
POOL_GROUP = D_MODEL // len(POOL_WINDOWS)
D_FF = 4 * D_MODEL
EPS = 1e-6

LANES = 128
QK_PAD = 2 * LANES
V_ROWS = V_HEAD + 16
POOL_HALO = 16
VMEM_LIMIT_BYTES = 56 * 1024 * 1024

F32 = jnp.float32
BF16 = jnp.bfloat16

ROW_TILE = 512
DOWN_SUB_ROWS = 128
ATTN_TILE = 512
FLASH_UNROLL = 14
MLP_ROW_TILE = 512
MLP_FF_TILE = 2048
MLP_FF_SUB = 1024


def _params(*semantics):
    return pltpu.CompilerParams(dimension_semantics=semantics,
                                vmem_limit_bytes=VMEM_LIMIT_BYTES)


def _rms(x, g):
    ms = jnp.mean(x * x, axis=-1, keepdims=True)
    return x * lax.rsqrt(ms + EPS) * g


def _rope_tables(pos, inv_freq):
    ang = pos.astype(F32) * inv_freq
    cos = jnp.cos(ang)
    sin = jnp.sin(ang)
    lane = lax.broadcasted_iota(jnp.int32, ang.shape, 1)
    first_half = (lane & (QK_ROPE // 2)) == 0
    return cos, jnp.where(first_half, -sin, 0.0), jnp.where(first_half, 0.0, sin)


def _rope(x, cos, sin_lo, sin_hi):
    half = QK_ROPE // 2
    return (x * cos + pltpu.roll(x, LANES - half, 1) * sin_lo
            + pltpu.roll(x, half, 1) * sin_hi)


def _attn_down_kernel(x_ref, g_ref, w_ref, qn_ref, kvn_ref, pos_ref, invf_ref,
                      cq_ref, ckv_ref, kr_ref):
    for r in range(ROW_TILE // DOWN_SUB_ROWS):
        rs = slice(r * DOWN_SUB_ROWS, (r + 1) * DOWN_SUB_ROWS)
        xn = _rms(x_ref[rs, :], g_ref[...]).astype(BF16)
        c = jnp.dot(xn, w_ref[...], preferred_element_type=F32)
        cq_ref[rs, :] = _rms(c[:, :Q_LORA], qn_ref[...]).astype(BF16)
        ckv_ref[rs, :] = _rms(c[:, Q_LORA:Q_LORA + KV_LORA], kvn_ref[...]).astype(BF16)
        cos, sin_lo, sin_hi = _rope_tables(pos_ref[rs, :], invf_ref[...])
        kr_ref[rs, :] = _rope(c[:, Q_LORA + KV_LORA:], cos, sin_lo, sin_hi).astype(BF16)


def _attn_down(x, g, w, qn, kvn, pos, invf):
    t = x.shape[0]
    n = w.shape[1]
    row = lambda i: (i, 0)
    fixed = lambda i: (0, 0)
    return pl.pallas_call(
        _attn_down_kernel,
        grid=(t // ROW_TILE,),
        in_specs=[
            pl.BlockSpec((ROW_TILE, D_MODEL), row),
            pl.BlockSpec((1, D_MODEL), fixed),
            pl.BlockSpec((D_MODEL, n), fixed),
            pl.BlockSpec((1, Q_LORA), fixed),
            pl.BlockSpec((1, KV_LORA), fixed),
            pl.BlockSpec((ROW_TILE, LANES), row),
            pl.BlockSpec((1, LANES), fixed),
        ],
        out_specs=[
            pl.BlockSpec((ROW_TILE, Q_LORA), row),
            pl.BlockSpec((ROW_TILE, KV_LORA), row),
            pl.BlockSpec((ROW_TILE, LANES), row),
        ],
        out_shape=[
            jax.ShapeDtypeStruct((t, Q_LORA), BF16),
            jax.ShapeDtypeStruct((t, KV_LORA), BF16),
            jax.ShapeDtypeStruct((t, LANES), BF16),
        ],
        compiler_params=_params("parallel"),
        name="attn_down",
    )(x, g, w, qn, kvn, pos, invf)


def _attn_up_kernel(cq_ref, ckv_ref, kr_ref, pos_ref, invf_ref, wqt_ref, wk_ref, wvt_ref,
                    qt_ref, k_ref, vt_ref, *, scale):
    cq = cq_ref[...]
    ckv = ckv_ref[...]
    kr = kr_ref[...]
    half = QK_ROPE // 2
    ang = invf_ref[...] * pos_ref[...].astype(F32)
    cos = jnp.cos(ang)
    sin = jnp.sin(ang)
    zeros = jnp.zeros((QK_PAD - QK_HEAD, ATTN_TILE), BF16)
    group = 4
    for c in range(N_HEADS // group):
        qt = lax.dot_general(wqt_ref[c * group * QK_HEAD:(c + 1) * group * QK_HEAD, :], cq,
                             (((1,), (1,)), ((), ())), preferred_element_type=F32)
        for hh in range(group):
            h = c * group + hh
            qh = qt[hh * QK_HEAD:(hh + 1) * QK_HEAD, :]
            x1 = qh[QK_NOPE:QK_NOPE + half, :]
            x2 = qh[QK_NOPE + half:, :]
            qt_ref[h, :QK_NOPE, :] = (qh[:QK_NOPE, :] * scale).astype(BF16)
            qt_ref[h, QK_NOPE:QK_NOPE + half, :] = ((x1 * cos - x2 * sin) * scale).astype(BF16)
            qt_ref[h, QK_NOPE + half:QK_HEAD, :] = ((x2 * cos + x1 * sin) * scale).astype(BF16)
            qt_ref[h, QK_HEAD:, :] = zeros
            k_ref[h, :, QK_NOPE:] = kr
    for p in range(N_HEADS // 2):
        kk = jnp.dot(ckv, wk_ref[:, p * QK_PAD:(p + 1) * QK_PAD], preferred_element_type=F32)
        k_ref[2 * p, :, :QK_NOPE] = kk[:, :QK_NOPE].astype(BF16)
        k_ref[2 * p + 1, :, :QK_NOPE] = kk[:, QK_NOPE:].astype(BF16)
    pad_rows = V_ROWS - V_HEAD
    ones_row = (lax.broadcasted_iota(jnp.int32, (pad_rows, ATTN_TILE), 0) == 0).astype(BF16)
    group = 4
    for c in range(N_HEADS // group):
        vt = lax.dot_general(wvt_ref[c * group * V_HEAD:(c + 1) * group * V_HEAD, :], ckv,
                             (((1,), (1,)), ((), ())), preferred_element_type=F32)
        for hh in range(group):
            vt_ref[c * group + hh, :V_HEAD, :] = vt[hh * V_HEAD:(hh + 1) * V_HEAD, :].astype(BF16)
            vt_ref[c * group + hh, V_HEAD:, :] = ones_row


def _attn_up(cq, ckv, kr, pos_rows, invf_col, wqt, wk, wvt, *, batch, seq, scale):
    t = cq.shape[0]
    tiles_per_seq = seq // ATTN_TILE
    row = lambda i: (i, 0)
    fixed = lambda i: (0, 0)
    head_major = lambda i: (i // tiles_per_seq, 0, i % tiles_per_seq, 0)
    chunk_major = lambda i: (i // tiles_per_seq, i % tiles_per_seq, 0, 0, 0)
    return pl.pallas_call(
        functools.partial(_attn_up_kernel, scale=scale),
        grid=(t // ATTN_TILE,),
        in_specs=[
            pl.BlockSpec((ATTN_TILE, Q_LORA), row),
            pl.BlockSpec((ATTN_TILE, KV_LORA), row),
            pl.BlockSpec((ATTN_TILE, LANES), row),
            pl.BlockSpec((None, 1, ATTN_TILE), lambda i: (i, 0, 0)),
            pl.BlockSpec(invf_col.shape, fixed),
            pl.BlockSpec(wqt.shape, fixed),
            pl.BlockSpec(wk.shape, fixed),
            pl.BlockSpec(wvt.shape, fixed),
        ],
        out_specs=[
            pl.BlockSpec((None, None, N_HEADS, QK_PAD, ATTN_TILE), chunk_major),
            pl.BlockSpec((None, N_HEADS, ATTN_TILE, QK_PAD), head_major),
            pl.BlockSpec((None, None, N_HEADS, V_ROWS, ATTN_TILE), chunk_major),
        ],
        out_shape=[
            jax.ShapeDtypeStruct((batch, tiles_per_seq, N_HEADS, QK_PAD, ATTN_TILE), BF16),
            jax.ShapeDtypeStruct((batch, N_HEADS, seq, QK_PAD), BF16),
            jax.ShapeDtypeStruct((batch, tiles_per_seq, N_HEADS, V_ROWS, ATTN_TILE), BF16),
        ],
        compiler_params=_params("parallel"),
        name="attn_up",
    )(cq, ckv, kr, pos_rows, invf_col, wqt, wk, wvt)


def _flash_kernel(qt_ref, k_ref, vt_ref, wu_ref, wd_ref, o_ref, wu_bf_ref, wd_bf_ref,
                  s_sc, cmax_sc, m_sc, acc_sc, *, n_tiles):
    tile = ATTN_TILE
    half = tile // 2

    def rows(idx):
        if isinstance(idx, int):
            return pl.ds(idx * tile, tile)
        return pl.ds(pl.multiple_of(idx * tile, tile), tile)

    def produce(i, j, slot):
        s = jnp.dot(k_ref[rows(j), :], qt_ref[i], preferred_element_type=F32)
        s_sc[slot] = s
        cmax_sc[slot] = jnp.max(s, axis=0, keepdims=True)

    def consume(i, j, slot):
        m_prev = m_sc[i]
        m_new = jnp.maximum(m_prev, cmax_sc[slot])
        alpha = jnp.exp2(m_prev - m_new)
        p = jnp.exp2(s_sc[slot] - m_new).astype(BF16)
        m_sc[i] = m_new
        acc_sc[i] = alpha * acc_sc[i] + jnp.dot(vt_ref[j], p, preferred_element_type=F32)

    def produce_diagonal(i, slot):
        s_lo = jnp.dot(k_ref[pl.ds(i * tile, half), :], qt_ref[i],
                       preferred_element_type=F32)
        s_hi = jnp.dot(k_ref[pl.ds(i * tile + half, half), :], qt_ref[i, :, half:],
                       preferred_element_type=F32)
        kpos = lax.broadcasted_iota(jnp.int32, (half, half), 0)
        qpos = lax.broadcasted_iota(jnp.int32, (half, half), 1)
        causal = kpos <= qpos
        s_ll = jnp.where(causal, s_lo[:, :half], -jnp.inf)
        s_hh = jnp.where(causal, s_hi, -jnp.inf)
        s_sc[slot, :half, :half] = s_ll
        s_sc[slot, :half, half:] = s_lo[:, half:]
        s_sc[slot, half:, half:] = s_hh
        cmax_sc[slot, :, :half] = jnp.max(s_ll, axis=0, keepdims=True)
        cmax_sc[slot, :, half:] = jnp.maximum(jnp.max(s_lo[:, half:], axis=0, keepdims=True),
                                              jnp.max(s_hh, axis=0, keepdims=True))

    def consume_diagonal(i, slot):
        m = cmax_sc[slot]
        p_lo = jnp.exp2(s_sc[slot, :half, :] - m).astype(BF16)
        p_hi = jnp.exp2(s_sc[slot, half:, half:] - m[:, half:]).astype(BF16)
        pv = jnp.dot(vt_ref[i, :, :half], p_lo, preferred_element_type=F32)
        pv_hi = jnp.dot(vt_ref[i, :, half:], p_hi, preferred_element_type=F32)
        m_sc[i] = m
        acc_sc[i, :, :half] = pv[:, :half]
        acc_sc[i, :, half:] = pv[:, half:] + pv_hi

    assert n_tiles % 2 == 0
    produce_diagonal(0, 0)
    for i in range(n_tiles):
        if i + 1 < n_tiles:
            produce_diagonal(i + 1, (i + 1) % 2)
        else:
            produce(1, 0, (i + 1) % 2)
        consume_diagonal(i, i % 2)

    def succ(i, j):
        wrap = j + 1 >= i
        return jnp.where(wrap, i + 1, i), jnp.where(wrap, 0, j + 1)

    n_off = n_tiles * (n_tiles - 1) // 2
    n_trips = n_off // FLASH_UNROLL
    assert n_off % FLASH_UNROLL == 0 and FLASH_UNROLL % 2 == 0
    wu_rows = wu_ref.shape[0] // n_trips
    wd_rows = wd_ref.shape[0] // n_trips
    assert wu_rows % 16 == 0 and wd_rows % 16 == 0

    def body(trip, cur):
        wu_rs = pl.ds(pl.multiple_of(trip * wu_rows, wu_rows), wu_rows)
        wd_rs = pl.ds(pl.multiple_of(trip * wd_rows, wd_rows), wd_rows)
        wu_bf_ref[wu_rs, :] = wu_ref[wu_rs, :].astype(BF16)
        wd_bf_ref[wd_rs, :] = wd_ref[wd_rs, :].astype(BF16)
        for u in range(FLASH_UNROLL):
            nxt = succ(*cur)
            produce(jnp.minimum(nxt[0], n_tiles - 1), jnp.minimum(nxt[1], n_tiles - 2),
                    1 - u % 2)
            consume(*cur, u % 2)
            cur = nxt
        return cur

    lax.fori_loop(0, n_trips, body, (jnp.int32(1), jnp.int32(0)))

    for i in range(n_tiles):
        acc = acc_sc[i]
        o = acc[:V_HEAD, :] * (1.0 / acc[V_HEAD:V_HEAD + 1, :])
        o_ref[i * tile:(i + 1) * tile, :] = o.T.astype(BF16)


def _flash(q, k, vt, w_up, w_down, *, batch, seq):
    n_tiles = seq // ATTN_TILE
    steps = batch * N_HEADS
    wu2 = w_up.reshape(-1, w_up.shape[-1])
    wd2 = w_down.reshape(-1, w_down.shape[-1])
    slab = lambda w: (w.shape[0] // steps, w.shape[1])
    assert wu2.shape[0] % (16 * steps) == 0 and wd2.shape[0] % (16 * steps) == 0
    step_row = lambda b, h: (b * N_HEADS + h, 0)
    o, wu_bf, wd_bf = pl.pallas_call(
        functools.partial(_flash_kernel, n_tiles=n_tiles),
        grid=(batch, N_HEADS),
        in_specs=[
            pl.BlockSpec((None, n_tiles, None, QK_PAD, ATTN_TILE), lambda b, h: (b, 0, h, 0, 0)),
            pl.BlockSpec((None, None, seq, QK_PAD), lambda b, h: (b, h, 0, 0)),
            pl.BlockSpec((None, n_tiles, None, V_ROWS, ATTN_TILE), lambda b, h: (b, 0, h, 0, 0)),
            pl.BlockSpec(slab(wu2), step_row),
            pl.BlockSpec(slab(wd2), step_row),
        ],
        out_specs=[
            pl.BlockSpec((None, None, seq, V_HEAD), lambda b, h: (b, h, 0, 0)),
            pl.BlockSpec(slab(wu2), step_row),
            pl.BlockSpec(slab(wd2), step_row),
        ],
        out_shape=[
            jax.ShapeDtypeStruct((batch, N_HEADS, seq, V_HEAD), BF16),
            jax.ShapeDtypeStruct(wu2.shape, BF16),
            jax.ShapeDtypeStruct(wd2.shape, BF16),
        ],
        scratch_shapes=[
            pltpu.VMEM((2, ATTN_TILE, ATTN_TILE), F32),
            pltpu.VMEM((2, 1, ATTN_TILE), F32),
            pltpu.VMEM((n_tiles, 1, ATTN_TILE), F32),
            pltpu.VMEM((n_tiles, V_ROWS, ATTN_TILE), F32),
        ],
        compiler_params=_params("parallel", "parallel"),
        name="flash_attn",
    )(q, k, vt, wu2, wd2)
    return o, wu_bf.reshape(w_up.shape), wd_bf.reshape(w_down.shape)


def _out_proj_kernel(x_ref, o_ref, w_ref, h_ref):
    o = jnp.concatenate([o_ref[h] for h in range(N_HEADS)], axis=-1)
    h_ref[...] = x_ref[...] + jnp.dot(o, w_ref[...], preferred_element_type=F32)


def _out_proj(x, o, w, *, seq):
    t = x.shape[0]
    tiles_per_seq = seq // ROW_TILE
    row = lambda i: (i, 0)
    return pl.pallas_call(
        _out_proj_kernel,
        grid=(t // ROW_TILE,),
        in_specs=[
            pl.BlockSpec((ROW_TILE, D_MODEL), row),
            pl.BlockSpec((None, N_HEADS, ROW_TILE, V_HEAD),
                         lambda i: (i // tiles_per_seq, 0, i % tiles_per_seq, 0)),
            pl.BlockSpec(w.shape, lambda i: (0, 0)),
        ],
        out_specs=pl.BlockSpec((ROW_TILE, D_MODEL), row),
        out_shape=jax.ShapeDtypeStruct((t, D_MODEL), F32),
        compiler_params=_params("parallel"),
        name="out_proj",
    )(x, o, w)


def _mlp_kernel(x_ref, g_ref, wu_ref, wd_ref, fg_ref, o_ref, xn_sc, *, final_norm):
    j = pl.program_id(1)

    @pl.when(j == 0)
    def _():
        x = x_ref[...]
        xn_sc[...] = _rms(x, g_ref[...]).astype(BF16)
        o_ref[...] = x

    for c in range(MLP_FF_TILE // MLP_FF_SUB):
        cs = slice(c * MLP_FF_SUB, (c + 1) * MLP_FF_SUB)
        u = jnp.dot(xn_sc[...], wu_ref[:, cs], preferred_element_type=F32)
        u = jnp.maximum(u, 0.0)
        o_ref[...] += jnp.dot((u * u).astype(BF16), wd_ref[cs, :], preferred_element_type=F32)

    if final_norm:
        @pl.when(j == pl.num_programs(1) - 1)
        def _():
            o_ref[...] = _rms(o_ref[...], fg_ref[...])


def _mlp(x, g, wu, wd, fg, *, layer, final_norm):
    t = x.shape[0]
    row = lambda i, j: (i, 0)
    fixed = lambda i, j: (0, 0)
    return pl.pallas_call(
        functools.partial(_mlp_kernel, final_norm=final_norm),
        grid=(t // MLP_ROW_TILE, D_FF // MLP_FF_TILE),
        in_specs=[
            pl.BlockSpec((MLP_ROW_TILE, D_MODEL), row),
            pl.BlockSpec((1, D_MODEL), fixed),
            pl.BlockSpec((None, D_MODEL, MLP_FF_TILE), lambda i, j: (layer, 0, j)),
            pl.BlockSpec((None, MLP_FF_TILE, D_MODEL), lambda i, j: (layer, j, 0)),
            pl.BlockSpec((1, D_MODEL), fixed),
        ],
        out_specs=pl.BlockSpec((MLP_ROW_TILE, D_MODEL), row),
        out_shape=jax.ShapeDtypeStruct((t, D_MODEL), F32),
        scratch_shapes=[pltpu.VMEM((MLP_ROW_TILE, D_MODEL), BF16)],
        compiler_params=_params("parallel", "arbitrary"),
        name="mlp_final" if final_norm else "mlp",
    )(x, g, wu, wd, fg)


def _pool_mix_kernel(x_ref, halo_ref, g_ref, w_ref, ps_ref, o_ref, buf_sc, *, tiles_per_seq):
    tile = ROW_TILE
    i = pl.program_id(0)
    x = x_ref[...]
    g = g_ref[...]
    xn = _rms(x, g)
    t_in_seq = i % tiles_per_seq
    halo = jnp.where(t_in_seq == 0, 0.0, _rms(halo_ref[...], g))
    buf_sc[:POOL_HALO, :] = halo
    buf_sc[POOL_HALO:, :] = xn
    t = t_in_seq * tile + lax.broadcasted_iota(jnp.int32, (tile, 1), 0)
    for gi, win in enumerate(POOL_WINDOWS):
        cols = slice(gi * POOL_GROUP, (gi + 1) * POOL_GROUP)
        xg = xn[:, cols]
        acc = buf_sc[:, cols]
        span = 1
        while span < win:
            acc = acc + pltpu.roll(acc, span, 0)
            span *= 2
        acc = acc[POOL_HALO:, :]
        cnt = jnp.minimum(t + 1, win).astype(F32)
        y = (acc / cnt - xg).astype(BF16)
        mix = jnp.dot(y, w_ref[gi], preferred_element_type=F32)
        o_ref[:, cols] = x[:, cols] + mix * ps_ref[:, cols]


def _pool_mix(x, g, w, ps, *, seq):
    t = x.shape[0]
    tiles_per_seq = seq // ROW_TILE
    halo_blocks = ROW_TILE // POOL_HALO
    row = lambda i: (i, 0)
    fixed = lambda i: (0, 0)
    return pl.pallas_call(
        functools.partial(_pool_mix_kernel, tiles_per_seq=tiles_per_seq),
        grid=(t // ROW_TILE,),
        in_specs=[
            pl.BlockSpec((ROW_TILE, D_MODEL), row),
            pl.BlockSpec((POOL_HALO, D_MODEL), lambda i: (jnp.maximum(i * halo_blocks - 1, 0), 0)),
            pl.BlockSpec((1, D_MODEL), fixed),
            pl.BlockSpec(w.shape, lambda i: (0, 0, 0)),
            pl.BlockSpec((1, D_MODEL), fixed),
        ],
        out_specs=pl.BlockSpec((ROW_TILE, D_MODEL), row),
        out_shape=jax.ShapeDtypeStruct((t, D_MODEL), F32),
        scratch_shapes=[pltpu.VMEM((ROW_TILE + POOL_HALO, D_MODEL), F32)],
        compiler_params=_params("parallel"),
        name="pool_mix",
    )(x, x, g, w, ps)


def kernel(x, positions, ln_mix, ln_mlp, w_dqkv, q_norm, kv_norm, w_uq, w_ukv, w_o,
           w_pool, pool_scale, w_up, w_down, final_norm):
    batch, seq, d = x.shape
    assert d == D_MODEL and seq % ATTN_TILE == 0 and seq % ROW_TILE == 0
    t = batch * seq
    h = x.reshape(t, d)

    w_dqkv_p = jnp.pad(w_dqkv[0], ((0, 0), (0, LANES - QK_ROPE))).astype(BF16)
    wqt = w_uq[0].T.astype(BF16)
    wkv = w_ukv[0].reshape(KV_LORA, N_HEADS, QK_NOPE + V_HEAD)
    wk = wkv[:, :, :QK_NOPE].reshape(KV_LORA, N_HEADS * QK_NOPE).astype(BF16)
    wvt = wkv[:, :, QK_NOPE:].reshape(KV_LORA, N_HEADS * V_HEAD).T.astype(BF16)
    wo = w_o[0].astype(BF16)
    wp = w_pool[0].astype(BF16)

    inv_freq = 1.0 / (ROPE_THETA ** (jnp.arange(0, QK_ROPE, 2, dtype=F32) / QK_ROPE))
    invf = jnp.concatenate([inv_freq, inv_freq, jnp.zeros((LANES - QK_ROPE,), F32)]).reshape(1, LANES)
    pos = jnp.broadcast_to(positions.reshape(t, 1), (t, LANES))
    pos_rows = positions.reshape(t // ATTN_TILE, 1, ATTN_TILE)
    invf_col = inv_freq.reshape(QK_ROPE // 2, 1)
    scale = math.log2(math.e) / math.sqrt(QK_HEAD)

    row = lambda v: v.reshape(1, -1)

    cq, ckv, kr = _attn_down(h, row(ln_mix[0]), w_dqkv_p, row(q_norm[0]), row(kv_norm[0]), pos, invf)
    qt, k, vt = _attn_up(cq, ckv, kr, pos_rows, invf_col, wqt, wk, wvt,
                         batch=batch, seq=seq, scale=scale)
    o, wu, wd = _flash(qt, k, vt, w_up, w_down, batch=batch, seq=seq)
    h = _out_proj(h, o, wo, seq=seq)
    h = _mlp(h, row(ln_mlp[0]), wu, wd, row(final_norm), layer=0, final_norm=False)

    h = _pool_mix(h, row(ln_mix[1]), wp, row(pool_scale[0]), seq=seq)
    h = _mlp(h, row(ln_mlp[1]), wu, wd, row(final_norm), layer=1, final_norm=True)
    return h.reshape(batch, seq, d)
```

```python
import functools
import math

import jax
import jax.numpy as jnp
from jax import lax
from jax.experimental import pallas as pl
from jax.experimental.pallas import tpu as pltpu

D_MODEL = 2048
N_HEADS = 16
Q_LORA = 512
KV_LORA = 512
QK_NOPE = 128
QK_ROPE = 64
V_HEAD = 128
QK_HEAD = QK_NOPE + QK_ROPE
ROPE_THETA = 10000.0
POOL_WINDOWS = (2, 4, 8, 16)
POOL_GROUP = D_MODEL // len(POOL_WINDOWS)
D_FF = 4 * D_MODEL
EPS = 1e-6

LANES = 128
BF16_ROWS = 16
QK_PAD = 2 * LANES
V_ROWS = V_HEAD + 16
POOL_HALO = 16
VMEM_LIMIT_BYTES = 56 * 1024 * 1024

F32 = jnp.float32
BF16 = jnp.bfloat16

ROW_TILE = 512
DOWN_ROW_TILE = 1024
DOWN_SUB_ROWS = 128
POOL_ROW_TILE = 1024
ATTN_TILE = 512
FLASH_UNROLL = 14
MLP_ROW_TILE = 512
MLP_FF_TILE = 2048
MLP_FF_SUB = 1024


def _params(*semantics):
    return pltpu.CompilerParams(dimension_semantics=semantics,
                                vmem_limit_bytes=VMEM_LIMIT_BYTES)


def _rms(x, g):
    ms = jnp.mean(x * x, axis=-1, keepdims=True)
    return x * lax.rsqrt(ms + EPS) * g


def _rope_tables(pos_row, invf_col):
    ang = invf_col * pos_row.astype(F32)
    cos = jnp.cos(ang)
    sin = jnp.sin(ang)
    z = jnp.zeros_like(cos)
    pad = jnp.zeros((LANES - QK_ROPE, ang.shape[1]), F32)
    cos_t = jnp.concatenate([cos, cos, pad], axis=0)
    sin_lo_t = jnp.concatenate([-sin, z, pad], axis=0)
    sin_hi_t = jnp.concatenate([z, sin, pad], axis=0)
    return cos_t.T, sin_lo_t.T, sin_hi_t.T


def _rope(x, cos, sin_lo, sin_hi):
    half = QK_ROPE // 2
    return (x * cos + pltpu.roll(x, LANES - half, 1) * sin_lo
            + pltpu.roll(x, half, 1) * sin_hi)


def _attn_down_kernel(x_ref, g_ref, w_ref, qn_ref, kvn_ref, pos_ref, invf_ref,
                      cq_ref, ckv_ref, kr_ref):
    cos, sin_lo, sin_hi = _rope_tables(pos_ref[...], invf_ref[...])
    for r in range(DOWN_ROW_TILE // DOWN_SUB_ROWS):
        rs = slice(r * DOWN_SUB_ROWS, (r + 1) * DOWN_SUB_ROWS)
        xn = _rms(x_ref[rs, :], g_ref[...]).astype(BF16)
        c = jnp.dot(xn, w_ref[...], preferred_element_type=F32)
        cq_ref[rs, :] = _rms(c[:, :Q_LORA], qn_ref[...]).astype(BF16)
        ckv_ref[rs, :] = _rms(c[:, Q_LORA:Q_LORA + KV_LORA], kvn_ref[...]).astype(BF16)
        kr_ref[rs, :] = _rope(c[:, Q_LORA + KV_LORA:], cos[rs, :], sin_lo[rs, :],
                              sin_hi[rs, :]).astype(BF16)


def _attn_down(x, g, w, qn, kvn, positions, invf_col):
    t = x.shape[0]
    n = w.shape[1]
    row = lambda i: (i, 0)
    fixed = lambda i: (0, 0)
    pos_rows = positions.reshape(t // DOWN_ROW_TILE, 1, DOWN_ROW_TILE)
    return pl.pallas_call(
        _attn_down_kernel,
        grid=(t // DOWN_ROW_TILE,),
        in_specs=[
            pl.BlockSpec((DOWN_ROW_TILE, D_MODEL), row),
            pl.BlockSpec((1, D_MODEL), fixed),
            pl.BlockSpec((D_MODEL, n), fixed),
            pl.BlockSpec((1, Q_LORA), fixed),
            pl.BlockSpec((1, KV_LORA), fixed),
            pl.BlockSpec((None, 1, DOWN_ROW_TILE), lambda i: (i, 0, 0)),
            pl.BlockSpec(invf_col.shape, fixed),
        ],
        out_specs=[
            pl.BlockSpec((DOWN_ROW_TILE, Q_LORA), row),
            pl.BlockSpec((DOWN_ROW_TILE, KV_LORA), row),
            pl.BlockSpec((DOWN_ROW_TILE, LANES), row),
        ],
        out_shape=[
            jax.ShapeDtypeStruct((t, Q_LORA), BF16),
            jax.ShapeDtypeStruct((t, KV_LORA), BF16),
            jax.ShapeDtypeStruct((t, LANES), BF16),
        ],
        compiler_params=_params("parallel"),
        name="attn_down",
    )(x, g, w, qn, kvn, pos_rows, invf_col)


def _attn_up_kernel(cq_ref, ckv_ref, kr_ref, pos_ref, invf_ref, wqt_ref, wk_ref, wvt_ref,
                    qt_ref, k_ref, vt_ref, *, scale):
    cq = cq_ref[...]
    ckv = ckv_ref[...]
    kr = kr_ref[...]
    half = QK_ROPE // 2
    ang = invf_ref[...] * pos_ref[...].astype(F32)
    cos = jnp.cos(ang)
    sin = jnp.sin(ang)
    zeros = jnp.zeros((QK_PAD - QK_HEAD, ATTN_TILE), BF16)
    group = 4
    for c in range(N_HEADS // group):
        qt = lax.dot_general(wqt_ref[c * group * QK_HEAD:(c + 1) * group * QK_HEAD, :], cq,
                             (((1,), (1,)), ((), ())), preferred_element_type=F32)
        for hh in range(group):
            h = c * group + hh
            qh = qt[hh * QK_HEAD:(hh + 1) * QK_HEAD, :]
            x1 = qh[QK_NOPE:QK_NOPE + half, :]
            x2 = qh[QK_NOPE + half:, :]
            qt_ref[h, :QK_NOPE, :] = (qh[:QK_NOPE, :] * scale).astype(BF16)
            qt_ref[h, QK_NOPE:QK_NOPE + half, :] = ((x1 * cos - x2 * sin) * scale).astype(BF16)
            qt_ref[h, QK_NOPE + half:QK_HEAD, :] = ((x2 * cos + x1 * sin) * scale).astype(BF16)
            qt_ref[h, QK_HEAD:, :] = zeros
            k_ref[h, :, QK_NOPE:] = kr
    for p in range(N_HEADS // 2):
        kk = jnp.dot(ckv, wk_ref[:, p * QK_PAD:(p + 1) * QK_PAD], preferred_element_type=F32)
        k_ref[2 * p, :, :QK_NOPE] = kk[:, :QK_NOPE].astype(BF16)
        k_ref[2 * p + 1, :, :QK_NOPE] = kk[:, QK_NOPE:].astype(BF16)
    pad_rows = V_ROWS - V_HEAD
    ones_row = (lax.broadcasted_iota(jnp.int32, (pad_rows, ATTN_TILE), 0) == 0).astype(BF16)
    group = 4
    for c in range(N_HEADS // group):
        vt = lax.dot_general(wvt_ref[c * group * V_HEAD:(c + 1) * group * V_HEAD, :], ckv,
                             (((1,), (1,)), ((), ())), preferred_element_type=F32)
        for hh in range(group):
            vt_ref[c * group + hh, :V_HEAD, :] = vt[hh * V_HEAD:(hh + 1) * V_HEAD, :].astype(BF16)
            vt_ref[c * group + hh, V_HEAD:, :] = ones_row


def _attn_up(cq, ckv, kr, pos_rows, invf_col, wqt, wk, wvt, *, batch, seq, scale):
    t = cq.shape[0]
    tiles_per_seq = seq // ATTN_TILE
    row = lambda i: (i, 0)
    fixed = lambda i: (0, 0)
    head_major = lambda i: (i // tiles_per_seq, 0, i % tiles_per_seq, 0)
    chunk_major = lambda i: (i // tiles_per_seq, i % tiles_per_seq, 0, 0, 0)
    return pl.pallas_call(
        functools.partial(_attn_up_kernel, scale=scale),
        grid=(t // ATTN_TILE,),
        in_specs=[
            pl.BlockSpec((ATTN_TILE, Q_LORA), row),
            pl.BlockSpec((ATTN_TILE, KV_LORA), row),
            pl.BlockSpec((ATTN_TILE, LANES), row),
            pl.BlockSpec((None, 1, ATTN_TILE), lambda i: (i, 0, 0)),
            pl.BlockSpec(invf_col.shape, fixed),
            pl.BlockSpec(wqt.shape, fixed),
            pl.BlockSpec(wk.shape, fixed),
            pl.BlockSpec(wvt.shape, fixed),
        ],
        out_specs=[
            pl.BlockSpec((None, None, N_HEADS, QK_PAD, ATTN_TILE), chunk_major),
            pl.BlockSpec((None, N_HEADS, ATTN_TILE, QK_PAD), head_major),
            pl.BlockSpec((None, None, N_HEADS, V_ROWS, ATTN_TILE), chunk_major),
        ],
        out_shape=[
            jax.ShapeDtypeStruct((batch, tiles_per_seq, N_HEADS, QK_PAD, ATTN_TILE), BF16),
            jax.ShapeDtypeStruct((batch, N_HEADS, seq, QK_PAD), BF16),
            jax.ShapeDtypeStruct((batch, tiles_per_seq, N_HEADS, V_ROWS, ATTN_TILE), BF16),
        ],
        compiler_params=_params("parallel"),
        name="attn_up",
    )(cq, ckv, kr, pos_rows, invf_col, wqt, wk, wvt)


def _flash_kernel(qt_ref, k_ref, vt_ref, *rest, n_tiles, n_cast):
    w_refs = rest[:n_cast]
    o_ref = rest[n_cast]
    w_bf_refs = rest[n_cast + 1:2 * n_cast + 1]
    s_sc, cmax_sc, m_sc, acc_sc = rest[2 * n_cast + 1:]
    tile = ATTN_TILE
    half = tile // 2

    def rows(idx):
        if isinstance(idx, int):
            return pl.ds(idx * tile, tile)
        return pl.ds(pl.multiple_of(idx * tile, tile), tile)

    def produce(i, j, slot):
        s = jnp.dot(k_ref[rows(j), :], qt_ref[i], preferred_element_type=F32)
        s_sc[slot] = s
        cmax_sc[slot] = jnp.max(s, axis=0, keepdims=True)

    def consume(i, j, slot):
        m_prev = m_sc[i]
        m_new = jnp.maximum(m_prev, cmax_sc[slot])
        alpha = jnp.exp2(m_prev - m_new)
        p = jnp.exp2(s_sc[slot] - m_new).astype(BF16)
        m_sc[i] = m_new
        acc_sc[i] = alpha * acc_sc[i] + jnp.dot(vt_ref[j], p, preferred_element_type=F32)

    def produce_diagonal(i, slot):
        s_lo = jnp.dot(k_ref[pl.ds(i * tile, half), :], qt_ref[i],
                       preferred_element_type=F32)
        s_hi = jnp.dot(k_ref[pl.ds(i * tile + half, half), :], qt_ref[i, :, half:],
                       preferred_element_type=F32)
        kpos = lax.broadcasted_iota(jnp.int32, (half, half), 0)
        qpos = lax.broadcasted_iota(jnp.int32, (half, half), 1)
        causal = kpos <= qpos
        s_ll = jnp.where(causal, s_lo[:, :half], -jnp.inf)
        s_hh = jnp.where(causal, s_hi, -jnp.inf)
        s_sc[slot, :half, :half] = s_ll
        s_sc[slot, :half, half:] = s_lo[:, half:]
        s_sc[slot, half:, half:] = s_hh
        cmax_sc[slot, :, :half] = jnp.max(s_ll, axis=0, keepdims=True)
        cmax_sc[slot, :, half:] = jnp.maximum(jnp.max(s_lo[:, half:], axis=0, keepdims=True),
                                              jnp.max(s_hh, axis=0, keepdims=True))

    def consume_diagonal(i, slot):
        m = cmax_sc[slot]
        p_lo = jnp.exp2(s_sc[slot, :half, :] - m).astype(BF16)
        p_hi = jnp.exp2(s_sc[slot, half:, half:] - m[:, half:]).astype(BF16)
        pv = jnp.dot(vt_ref[i, :, :half], p_lo, preferred_element_type=F32)
        pv_hi = jnp.dot(vt_ref[i, :, half:], p_hi, preferred_element_type=F32)
        m_sc[i] = m
        acc_sc[i, :, :half] = pv[:, :half]
        acc_sc[i, :, half:] = pv[:, half:] + pv_hi

    assert n_tiles % 2 == 0
    produce_diagonal(0, 0)
    for i in range(n_tiles):
        if i + 1 < n_tiles:
            produce_diagonal(i + 1, (i + 1) % 2)
        else:
            produce(1, 0, (i + 1) % 2)
        consume_diagonal(i, i % 2)

    def succ(i, j):
        wrap = j + 1 >= i
        return jnp.where(wrap, i + 1, i), jnp.where(wrap, 0, j + 1)

    n_off = n_tiles * (n_tiles - 1) // 2
    n_trips = n_off // FLASH_UNROLL
    assert n_off % FLASH_UNROLL == 0 and FLASH_UNROLL % 2 == 0

    def body(trip, cur):
        for w_ref, w_bf_ref in zip(w_refs, w_bf_refs):
            n_rows = w_ref.shape[0] // n_trips
            assert n_rows % BF16_ROWS == 0
            rs = pl.ds(pl.multiple_of(trip * n_rows, n_rows), n_rows)
            w_bf_ref[rs, :] = w_ref[rs, :].astype(BF16)
        for u in range(FLASH_UNROLL):
            nxt = succ(*cur)
            produce(jnp.minimum(nxt[0], n_tiles - 1), jnp.minimum(nxt[1], n_tiles - 2),
                    1 - u % 2)
            consume(*cur, u % 2)
            cur = nxt
        return cur

    lax.fori_loop(0, n_trips, body, (jnp.int32(1), jnp.int32(0)))

    for i in range(n_tiles):
        acc = acc_sc[i]
        o = acc[:V_HEAD, :] * (1.0 / acc[V_HEAD:V_HEAD + 1, :])
        o_ref[i * tile:(i + 1) * tile, :] = o.T.astype(BF16)


def _flash(q, k, vt, weights, *, batch, seq):
    n_tiles = seq // ATTN_TILE
    steps = batch * N_HEADS
    w2d = [w.reshape(-1, w.shape[-1]) for w in weights]
    slabs = [(w.shape[0] // steps, w.shape[1]) for w in w2d]
    assert all(w.shape[0] % steps == 0 for w in w2d)
    step_row = lambda b, h: (b * N_HEADS + h, 0)
    o, *w_bf = pl.pallas_call(
        functools.partial(_flash_kernel, n_tiles=n_tiles, n_cast=len(w2d)),
        grid=(batch, N_HEADS),
        in_specs=[
            pl.BlockSpec((None, n_tiles, None, QK_PAD, ATTN_TILE), lambda b, h: (b, 0, h, 0, 0)),
            pl.BlockSpec((None, None, seq, QK_PAD), lambda b, h: (b, h, 0, 0)),
            pl.BlockSpec((None, n_tiles, None, V_ROWS, ATTN_TILE), lambda b, h: (b, 0, h, 0, 0)),
        ] + [pl.BlockSpec(slab, step_row) for slab in slabs],
        out_specs=[
            pl.BlockSpec((None, None, seq, V_HEAD), lambda b, h: (b, h, 0, 0)),
        ] + [pl.BlockSpec(slab, step_row) for slab in slabs],
        out_shape=[
            jax.ShapeDtypeStruct((batch, N_HEADS, seq, V_HEAD), BF16),
        ] + [jax.ShapeDtypeStruct(w.shape, BF16) for w in w2d],
        scratch_shapes=[
            pltpu.VMEM((2, ATTN_TILE, ATTN_TILE), F32),
            pltpu.VMEM((2, 1, ATTN_TILE), F32),
            pltpu.VMEM((n_tiles, 1, ATTN_TILE), F32),
            pltpu.VMEM((n_tiles, V_ROWS, ATTN_TILE), F32),
        ],
        compiler_params=_params("parallel", "parallel"),
        name="flash_attn",
    )(q, k, vt, *w2d)
    return o, [wb.reshape(w.shape) for wb, w in zip(w_bf, weights)]


def _out_proj_kernel(x_ref, o_ref, w_ref, h_ref):
    o = jnp.concatenate([o_ref[h] for h in range(N_HEADS)], axis=-1)
    h_ref[...] = x_ref[...] + jnp.dot(o, w_ref[...], preferred_element_type=F32)


def _out_proj(x, o, w, *, seq):
    t = x.shape[0]
    tiles_per_seq = seq // ROW_TILE
    row = lambda i: (i, 0)
    return pl.pallas_call(
        _out_proj_kernel,
        grid=(t // ROW_TILE,),
        in_specs=[
            pl.BlockSpec((ROW_TILE, D_MODEL), row),
            pl.BlockSpec((None, N_HEADS, ROW_TILE, V_HEAD),
                         lambda i: (i // tiles_per_seq, 0, i % tiles_per_seq, 0)),
            pl.BlockSpec(w.shape, lambda i: (0, 0)),
        ],
        out_specs=pl.BlockSpec((ROW_TILE, D_MODEL), row),
        out_shape=jax.ShapeDtypeStruct((t, D_MODEL), F32),
        compiler_params=_params("parallel"),
        name="out_proj",
    )(x, o, w)


def _mlp_kernel(x_ref, g_ref, wu_ref, wd_ref, fg_ref, o_ref, xn_sc, *, final_norm):
    j = pl.program_id(1)

    @pl.when(j == 0)
    def _():
        x = x_ref[...]
        xn_sc[...] = _rms(x, g_ref[...]).astype(BF16)
        o_ref[...] = x

    for c in range(MLP_FF_TILE // MLP_FF_SUB):
        cs = slice(c * MLP_FF_SUB, (c + 1) * MLP_FF_SUB)
        u = jnp.dot(xn_sc[...], wu_ref[:, cs], preferred_element_type=F32)
        u = jnp.maximum(u, 0.0)
        o_ref[...] += jnp.dot((u * u).astype(BF16), wd_ref[cs, :], preferred_element_type=F32)

    if final_norm:
        @pl.when(j == pl.num_programs(1) - 1)
        def _():
            o_ref[...] = _rms(o_ref[...], fg_ref[...])


def _mlp(x, g, wu, wd, fg, *, layer, final_norm):
    t = x.shape[0]
    row = lambda i, j: (i, 0)
    fixed = lambda i, j: (0, 0)
    return pl.pallas_call(
        functools.partial(_mlp_kernel, final_norm=final_norm),
        grid=(t // MLP_ROW_TILE, D_FF // MLP_FF_TILE),
        in_specs=[
            pl.BlockSpec((MLP_ROW_TILE, D_MODEL), row),
            pl.BlockSpec((1, D_MODEL), fixed),
            pl.BlockSpec((None, D_MODEL, MLP_FF_TILE), lambda i, j: (layer, 0, j)),
            pl.BlockSpec((None, MLP_FF_TILE, D_MODEL), lambda i, j: (layer, j, 0)),
            pl.BlockSpec((1, D_MODEL), fixed),
        ],
        out_specs=pl.BlockSpec((MLP_ROW_TILE, D_MODEL), row),
        out_shape=jax.ShapeDtypeStruct((t, D_MODEL), F32),
        scratch_shapes=[pltpu.VMEM((MLP_ROW_TILE, D_MODEL), BF16)],
        compiler_params=_params("parallel", "arbitrary"),
        name="mlp_final" if final_norm else "mlp",
    )(x, g, wu, wd, fg)


def _pool_mix_kernel(x_ref, halo_ref, g_ref, w_ref, ps_ref, o_ref, buf_sc, *, tiles_per_seq):
    tile = POOL_ROW_TILE
    i = pl.program_id(0)
    x = x_ref[...]
    g = g_ref[...]
    xn = _rms(x, g)
    t_in_seq = i % tiles_per_seq
    halo = jnp.where(t_in_seq == 0, 0.0, _rms(halo_ref[...], g))
    buf_sc[:POOL_HALO, :] = halo
    buf_sc[POOL_HALO:, :] = xn
    t = t_in_seq * tile + lax.broadcasted_iota(jnp.int32, (tile, 1), 0)
    for gi, win in enumerate(POOL_WINDOWS):
        cols = slice(gi * POOL_GROUP, (gi + 1) * POOL_GROUP)
        xg = xn[:, cols]
        acc = buf_sc[:, cols]
        span = 1
        while span < win:
            acc = acc + pltpu.roll(acc, span, 0)
            span *= 2
        acc = acc[POOL_HALO:, :]
        cnt = jnp.minimum(t + 1, win).astype(F32)
        y = (acc / cnt - xg).astype(BF16)
        mix = jnp.dot(y, w_ref[gi], preferred_element_type=F32)
        o_ref[:, cols] = x[:, cols] + mix * ps_ref[:, cols]


def _pool_mix(x, g, w, ps, *, seq):
    t = x.shape[0]
    tiles_per_seq = seq // POOL_ROW_TILE
    halo_blocks = POOL_ROW_TILE // POOL_HALO
    row = lambda i: (i, 0)
    fixed = lambda i: (0, 0)
    return pl.pallas_call(
        functools.partial(_pool_mix_kernel, tiles_per_seq=tiles_per_seq),
        grid=(t // POOL_ROW_TILE,),
        in_specs=[
            pl.BlockSpec((POOL_ROW_TILE, D_MODEL), row),
            pl.BlockSpec((POOL_HALO, D_MODEL), lambda i: (jnp.maximum(i * halo_blocks - 1, 0), 0)),
            pl.BlockSpec((1, D_MODEL), fixed),
            pl.BlockSpec(w.shape, lambda i: (0, 0, 0)),
            pl.BlockSpec((1, D_MODEL), fixed),
        ],
        out_specs=pl.BlockSpec((POOL_ROW_TILE, D_MODEL), row),
        out_shape=jax.ShapeDtypeStruct((t, D_MODEL), F32),
        scratch_shapes=[pltpu.VMEM((POOL_ROW_TILE + POOL_HALO, D_MODEL), F32)],
        compiler_params=_params("parallel"),
        name="pool_mix",
    )(x, x, g, w, ps)


def kernel(x, positions, ln_mix, ln_mlp, w_dqkv, q_norm, kv_norm, w_uq, w_ukv, w_o,
           w_pool, pool_scale, w_up, w_down, final_norm):
    batch, seq, d = x.shape
    assert d == D_MODEL and all(
        seq % tile == 0 for tile in (ATTN_TILE, ROW_TILE, DOWN_ROW_TILE, POOL_ROW_TILE))
    t = batch * seq
    h = x.reshape(t, d)

    w_dqkv_p = jnp.pad(w_dqkv[0], ((0, 0), (0, LANES - QK_ROPE))).astype(BF16)
    wqt = w_uq[0].T.astype(BF16)
    wkv = w_ukv[0].reshape(KV_LORA, N_HEADS, QK_NOPE + V_HEAD)
    wk = wkv[:, :, :QK_NOPE].reshape(KV_LORA, N_HEADS * QK_NOPE).astype(BF16)
    wvt = wkv[:, :, QK_NOPE:].reshape(KV_LORA, N_HEADS * V_HEAD).T.astype(BF16)
    wp = w_pool[0].astype(BF16)

    inv_freq = 1.0 / (ROPE_THETA ** (jnp.arange(0, QK_ROPE, 2, dtype=F32) / QK_ROPE))
    pos_rows = positions.reshape(t // ATTN_TILE, 1, ATTN_TILE)
    invf_col = inv_freq.reshape(QK_ROPE // 2, 1)
    scale = math.log2(math.e) / math.sqrt(QK_HEAD)

    row = lambda v: v.reshape(1, -1)

    cq, ckv, kr = _attn_down(h, row(ln_mix[0]), w_dqkv_p, row(q_norm[0]), row(kv_norm[0]),
                             positions, invf_col)
    qt, k, vt = _attn_up(cq, ckv, kr, pos_rows, invf_col, wqt, wk, wvt,
                         batch=batch, seq=seq, scale=scale)
    o, (wu, wd, wo) = _flash(qt, k, vt, (w_up, w_down, w_o[0]), batch=batch, seq=seq)
    h = _out_proj(h, o, wo, seq=seq)
    h = _mlp(h, row(ln_mlp[0]), wu, wd, row(final_norm), layer=0, final_norm=False)

    h = _pool_mix(h, row(ln_mix[1]), wp, row(pool_scale[0]), seq=seq)
    h = _mlp(h, row(ln_mlp[1]), wu, wd, row(final_norm), layer=1, final_norm=True)
    return h.reshape(batch, seq, d)
```

```python
import functools
import math

import jax
import jax.numpy as jnp
from jax import lax
from jax.experimental import pallas as pl
from jax.experimental.pallas import tpu as pltpu

D_MODEL = 2048
N_HEADS = 16
Q_LORA = 512
KV_LORA = 512
QK_NOPE = 128
QK_ROPE = 64
V_HEAD = 128
QK_HEAD = QK_NOPE + QK_ROPE
ROPE_THETA = 10000.0
POOL_WINDOWS = (2, 4, 8, 16)
POOL_GROUP = D_MODEL // len(POOL_WINDOWS)
D_FF = 4 * D_MODEL
EPS = 1e-6

LANES = 128
BF16_ROWS = 16
QK_PAD = 2 * LANES
V_ROWS = V_HEAD + 16
POOL_HALO = 16
VMEM_LIMIT_BYTES = 56 * 1024 * 1024

F32 = jnp.float32
BF16 = jnp.bfloat16

ROW_TILE = 512
DOWN_ROW_TILE = 1024
DOWN_SUB_ROWS = 128
POOL_ROW_TILE = 1024
ATTN_TILE = 512
FLASH_HEADS = 2
FLASH_UNROLL = 14
MLP_ROW_TILE = 512
MLP_FF_TILE = 2048
MLP_FF_SUB = 1024


def _params(*semantics):
    return pltpu.CompilerParams(dimension_semantics=semantics,
                                vmem_limit_bytes=VMEM_LIMIT_BYTES)


def _rms(x, g):
    ms = jnp.mean(x * x, axis=-1, keepdims=True)
    return x * lax.rsqrt(ms + EPS) * g


def _rope_tables(pos_row, invf_col):
    ang = invf_col * pos_row.astype(F32)
    cos = jnp.cos(ang)
    sin = jnp.sin(ang)
    z = jnp.zeros_like(cos)
    pad = jnp.zeros((LANES - QK_ROPE, ang.shape[1]), F32)
    cos_t = jnp.concatenate([cos, cos, pad], axis=0)
    sin_lo_t = jnp.concatenate([-sin, z, pad], axis=0)
    sin_hi_t = jnp.concatenate([z, sin, pad], axis=0)
    return cos_t.T, sin_lo_t.T, sin_hi_t.T


def _rope(x, cos, sin_lo, sin_hi):
    half = QK_ROPE // 2
    return (x * cos + pltpu.roll(x, LANES - half, 1) * sin_lo
            + pltpu.roll(x, half, 1) * sin_hi)


def _attn_down_kernel(x_ref, g_ref, w_ref, qn_ref, kvn_ref, pos_ref, invf_ref,
                      cq_ref, ckv_ref, kr_ref):
    cos, sin_lo, sin_hi = _rope_tables(pos_ref[...], invf_ref[...])
    for r in range(DOWN_ROW_TILE // DOWN_SUB_ROWS):
        rs = slice(r * DOWN_SUB_ROWS, (r + 1) * DOWN_SUB_ROWS)
        xn = _rms(x_ref[rs, :], g_ref[...]).astype(BF16)
        c = jnp.dot(xn, w_ref[...], preferred_element_type=F32)
        cq_ref[rs, :] = _rms(c[:, :Q_LORA], qn_ref[...]).astype(BF16)
        ckv_ref[rs, :] = _rms(c[:, Q_LORA:Q_LORA + KV_LORA], kvn_ref[...]).astype(BF16)
        kr_ref[rs, :] = _rope(c[:, Q_LORA + KV_LORA:], cos[rs, :], sin_lo[rs, :],
                              sin_hi[rs, :]).astype(BF16)


def _attn_down(x, g, w, qn, kvn, positions, invf_col):
    t = x.shape[0]
    n = w.shape[1]
    row = lambda i: (i, 0)
    fixed = lambda i: (0, 0)
    pos_rows = positions.reshape(t // DOWN_ROW_TILE, 1, DOWN_ROW_TILE)
    return pl.pallas_call(
        _attn_down_kernel,
        grid=(t // DOWN_ROW_TILE,),
        in_specs=[
            pl.BlockSpec((DOWN_ROW_TILE, D_MODEL), row),
            pl.BlockSpec((1, D_MODEL), fixed),
            pl.BlockSpec((D_MODEL, n), fixed),
            pl.BlockSpec((1, Q_LORA), fixed),
            pl.BlockSpec((1, KV_LORA), fixed),
            pl.BlockSpec((None, 1, DOWN_ROW_TILE), lambda i: (i, 0, 0)),
            pl.BlockSpec(invf_col.shape, fixed),
        ],
        out_specs=[
            pl.BlockSpec((DOWN_ROW_TILE, Q_LORA), row),
            pl.BlockSpec((DOWN_ROW_TILE, KV_LORA), row),
            pl.BlockSpec((DOWN_ROW_TILE, LANES), row),
        ],
        out_shape=[
            jax.ShapeDtypeStruct((t, Q_LORA), BF16),
            jax.ShapeDtypeStruct((t, KV_LORA), BF16),
            jax.ShapeDtypeStruct((t, LANES), BF16),
        ],
        compiler_params=_params("parallel"),
        name="attn_down",
    )(x, g, w, qn, kvn, pos_rows, invf_col)


def _attn_up_kernel(cq_ref, ckv_ref, kr_ref, pos_ref, invf_ref, wqt_ref, wk_ref, wvt_ref,
                    w_ref, qt_ref, k_ref, vt_ref, w_bf_ref, *, scale):
    w_bf_ref[...] = w_ref[...].astype(BF16)
    cq = cq_ref[...]
    ckv = ckv_ref[...]
    kr = kr_ref[...]
    half = QK_ROPE // 2
    ang = invf_ref[...] * pos_ref[...].astype(F32)
    cos = jnp.cos(ang)
    sin = jnp.sin(ang)
    zeros = jnp.zeros((QK_PAD - QK_HEAD, ATTN_TILE), BF16)
    group = 4
    for c in range(N_HEADS // group):
        qt = lax.dot_general(wqt_ref[c * group * QK_HEAD:(c + 1) * group * QK_HEAD, :], cq,
                             (((1,), (1,)), ((), ())), preferred_element_type=F32)
        for hh in range(group):
            h = c * group + hh
            qh = qt[hh * QK_HEAD:(hh + 1) * QK_HEAD, :]
            x1 = qh[QK_NOPE:QK_NOPE + half, :]
            x2 = qh[QK_NOPE + half:, :]
            qt_ref[h, :QK_NOPE, :] = (qh[:QK_NOPE, :] * scale).astype(BF16)
            qt_ref[h, QK_NOPE:QK_NOPE + half, :] = ((x1 * cos - x2 * sin) * scale).astype(BF16)
            qt_ref[h, QK_NOPE + half:QK_HEAD, :] = ((x2 * cos + x1 * sin) * scale).astype(BF16)
            qt_ref[h, QK_HEAD:, :] = zeros
            k_ref[h, :, QK_NOPE:] = kr
    for p in range(N_HEADS // 2):
        kk = jnp.dot(ckv, wk_ref[:, p * QK_PAD:(p + 1) * QK_PAD], preferred_element_type=F32)
        k_ref[2 * p, :, :QK_NOPE] = kk[:, :QK_NOPE].astype(BF16)
        k_ref[2 * p + 1, :, :QK_NOPE] = kk[:, QK_NOPE:].astype(BF16)
    pad_rows = V_ROWS - V_HEAD
    ones_row = (lax.broadcasted_iota(jnp.int32, (pad_rows, ATTN_TILE), 0) == 0).astype(BF16)
    group = 4
    for c in range(N_HEADS // group):
        vt = lax.dot_general(wvt_ref[c * group * V_HEAD:(c + 1) * group * V_HEAD, :], ckv,
                             (((1,), (1,)), ((), ())), preferred_element_type=F32)
        for hh in range(group):
            vt_ref[c * group + hh, :V_HEAD, :] = vt[hh * V_HEAD:(hh + 1) * V_HEAD, :].astype(BF16)
            vt_ref[c * group + hh, V_HEAD:, :] = ones_row


def _attn_up(cq, ckv, kr, pos_rows, invf_col, wqt, wk, wvt, w_cast, *, batch, seq, scale):
    t = cq.shape[0]
    tiles_per_seq = seq // ATTN_TILE
    steps = t // ATTN_TILE
    w2d = w_cast.reshape(-1, w_cast.shape[-1])
    assert w2d.shape[0] % (steps * BF16_ROWS) == 0
    slab = (w2d.shape[0] // steps, w2d.shape[1])
    row = lambda i: (i, 0)
    fixed = lambda i: (0, 0)
    head_major = lambda i: (i // tiles_per_seq, 0, i % tiles_per_seq, 0)
    chunk_major = lambda i: (i // tiles_per_seq, i % tiles_per_seq, 0, 0, 0)
    qt, k, vt, w_bf = pl.pallas_call(
        functools.partial(_attn_up_kernel, scale=scale),
        grid=(steps,),
        in_specs=[
            pl.BlockSpec((ATTN_TILE, Q_LORA), row),
            pl.BlockSpec((ATTN_TILE, KV_LORA), row),
            pl.BlockSpec((ATTN_TILE, LANES), row),
            pl.BlockSpec((None, 1, ATTN_TILE), lambda i: (i, 0, 0)),
            pl.BlockSpec(invf_col.shape, fixed),
            pl.BlockSpec(wqt.shape, fixed),
            pl.BlockSpec(wk.shape, fixed),
            pl.BlockSpec(wvt.shape, fixed),
            pl.BlockSpec(slab, row),
        ],
        out_specs=[
            pl.BlockSpec((None, None, N_HEADS, QK_PAD, ATTN_TILE), chunk_major),
            pl.BlockSpec((None, N_HEADS, ATTN_TILE, QK_PAD), head_major),
            pl.BlockSpec((None, None, N_HEADS, V_ROWS, ATTN_TILE), chunk_major),
            pl.BlockSpec(slab, row),
        ],
        out_shape=[
            jax.ShapeDtypeStruct((batch, tiles_per_seq, N_HEADS, QK_PAD, ATTN_TILE), BF16),
            jax.ShapeDtypeStruct((batch, N_HEADS, seq, QK_PAD), BF16),
            jax.ShapeDtypeStruct((batch, tiles_per_seq, N_HEADS, V_ROWS, ATTN_TILE), BF16),
            jax.ShapeDtypeStruct(w2d.shape, BF16),
        ],
        compiler_params=_params("parallel"),
        name="attn_up",
    )(cq, ckv, kr, pos_rows, invf_col, wqt, wk, wvt, w2d)
    return qt, k, vt, w_bf.reshape(w_cast.shape)


def _flash_kernel(qt_ref, k_ref, vt_ref, *rest, n_tiles, n_cast):
    w_refs = rest[:n_cast]
    o_ref = rest[n_cast]
    w_bf_refs = rest[n_cast + 1:2 * n_cast + 1]
    s_sc, cmax_sc, m_sc, acc_sc = rest[2 * n_cast + 1:]
    tile = ATTN_TILE
    half = tile // 2

    def rows(idx):
        if isinstance(idx, int):
            return pl.ds(idx * tile, tile)
        return pl.ds(pl.multiple_of(idx * tile, tile), tile)

    def produce(hh, i, j, slot):
        s = jnp.dot(k_ref[hh, rows(j), :], qt_ref[i, hh], preferred_element_type=F32)
        s_sc[hh, slot] = s
        cmax_sc[hh, slot] = jnp.max(s, axis=0, keepdims=True)

    def consume(hh, i, j, slot):
        m_prev = m_sc[hh, i]
        m_new = jnp.maximum(m_prev, cmax_sc[hh, slot])
        alpha = jnp.exp2(m_prev - m_new)
        p = jnp.exp2(s_sc[hh, slot] - m_new).astype(BF16)
        m_sc[hh, i] = m_new
        acc_sc[hh, i] = alpha * acc_sc[hh, i] + jnp.dot(vt_ref[j, hh], p,
                                                        preferred_element_type=F32)

    def produce_diagonal(hh, i, slot):
        s_lo = jnp.dot(k_ref[hh, pl.ds(i * tile, half), :], qt_ref[i, hh],
                       preferred_element_type=F32)
        s_hi = jnp.dot(k_ref[hh, pl.ds(i * tile + half, half), :], qt_ref[i, hh, :, half:],
                       preferred_element_type=F32)
        kpos = lax.broadcasted_iota(jnp.int32, (half, half), 0)
        qpos = lax.broadcasted_iota(jnp.int32, (half, half), 1)
        causal = kpos <= qpos
        s_ll = jnp.where(causal, s_lo[:, :half], -jnp.inf)
        s_hh = jnp.where(causal, s_hi, -jnp.inf)
        s_sc[hh, slot, :half, :half] = s_ll
        s_sc[hh, slot, :half, half:] = s_lo[:, half:]
        s_sc[hh, slot, half:, half:] = s_hh
        cmax_sc[hh, slot, :, :half] = jnp.max(s_ll, axis=0, keepdims=True)
        cmax_sc[hh, slot, :, half:] = jnp.maximum(
            jnp.max(s_lo[:, half:], axis=0, keepdims=True), jnp.max(s_hh, axis=0, keepdims=True))

    def consume_diagonal(hh, i, slot):
        m = cmax_sc[hh, slot]
        p_lo = jnp.exp2(s_sc[hh, slot, :half, :] - m).astype(BF16)
        p_hi = jnp.exp2(s_sc[hh, slot, half:, half:] - m[:, half:]).astype(BF16)
        pv = jnp.dot(vt_ref[i, hh, :, :half], p_lo, preferred_element_type=F32)
        pv_hi = jnp.dot(vt_ref[i, hh, :, half:], p_hi, preferred_element_type=F32)
        m_sc[hh, i] = m
        acc_sc[hh, i, :, :half] = pv[:, :half]
        acc_sc[hh, i, :, half:] = pv[:, half:] + pv_hi

    assert n_tiles % 2 == 0
    heads = range(FLASH_HEADS)
    for hh in heads:
        produce_diagonal(hh, 0, 0)
    for i in range(n_tiles):
        for hh in heads:
            if i + 1 < n_tiles:
                produce_diagonal(hh, i + 1, (i + 1) % 2)
            else:
                produce(hh, 1, 0, (i + 1) % 2)
        for hh in heads:
            consume_diagonal(hh, i, i % 2)

    def succ(i, j):
        wrap = j + 1 >= i
        return jnp.where(wrap, i + 1, i), jnp.where(wrap, 0, j + 1)

    n_off = n_tiles * (n_tiles - 1) // 2
    n_trips = n_off // FLASH_UNROLL
    assert n_off % FLASH_UNROLL == 0 and FLASH_UNROLL % 2 == 0

    def body(trip, cur):
        for w_ref, w_bf_ref in zip(w_refs, w_bf_refs):
            n_rows = w_ref.shape[0] // n_trips
            assert n_rows % BF16_ROWS == 0
            rs = pl.ds(pl.multiple_of(trip * n_rows, n_rows), n_rows)
            w_bf_ref[rs, :] = w_ref[rs, :].astype(BF16)
        for u in range(FLASH_UNROLL):
            nxt = succ(*cur)
            for hh in heads:
                produce(hh, jnp.minimum(nxt[0], n_tiles - 1), jnp.minimum(nxt[1], n_tiles - 2),
                        1 - u % 2)
            for hh in heads:
                consume(hh, *cur, u % 2)
            cur = nxt
        return cur

    lax.fori_loop(0, n_trips, body, (jnp.int32(1), jnp.int32(0)))

    for hh in heads:
        for i in range(n_tiles):
            acc = acc_sc[hh, i]
            o = acc[:V_HEAD, :] * (1.0 / acc[V_HEAD:V_HEAD + 1, :])
            o_ref[hh, i * tile:(i + 1) * tile, :] = o.T.astype(BF16)


def _flash(q, k, vt, weights, *, batch, seq):
    n_tiles = seq // ATTN_TILE
    head_groups = N_HEADS // FLASH_HEADS
    steps = batch * head_groups
    w2d = [w.reshape(-1, w.shape[-1]) for w in weights]
    slabs = [(w.shape[0] // steps, w.shape[1]) for w in w2d]
    assert all(w.shape[0] % steps == 0 for w in w2d)
    step_row = lambda b, g: (b * head_groups + g, 0)
    heads_of = lambda b, g: (b, g, 0, 0)
    chunked_heads_of = lambda b, g: (b, 0, g, 0, 0)
    o, *w_bf = pl.pallas_call(
        functools.partial(_flash_kernel, n_tiles=n_tiles, n_cast=len(w2d)),
        grid=(batch, head_groups),
        in_specs=[
            pl.BlockSpec((None, n_tiles, FLASH_HEADS, QK_PAD, ATTN_TILE), chunked_heads_of),
            pl.BlockSpec((None, FLASH_HEADS, seq, QK_PAD), heads_of),
            pl.BlockSpec((None, n_tiles, FLASH_HEADS, V_ROWS, ATTN_TILE), chunked_heads_of),
        ] + [pl.BlockSpec(slab, step_row) for slab in slabs],
        out_specs=[
            pl.BlockSpec((None, FLASH_HEADS, seq, V_HEAD), heads_of),
        ] + [pl.BlockSpec(slab, step_row) for slab in slabs],
        out_shape=[
            jax.ShapeDtypeStruct((batch, N_HEADS, seq, V_HEAD), BF16),
        ] + [jax.ShapeDtypeStruct(w.shape, BF16) for w in w2d],
        scratch_shapes=[
            pltpu.VMEM((FLASH_HEADS, 2, ATTN_TILE, ATTN_TILE), F32),
            pltpu.VMEM((FLASH_HEADS, 2, 1, ATTN_TILE), F32),
            pltpu.VMEM((FLASH_HEADS, n_tiles, 1, ATTN_TILE), F32),
            pltpu.VMEM((FLASH_HEADS, n_tiles, V_ROWS, ATTN_TILE), F32),
        ],
        compiler_params=_params("parallel", "parallel"),
        name="flash_attn",
    )(q, k, vt, *w2d)
    return o, [wb.reshape(w.shape) for wb, w in zip(w_bf, weights)]


def _out_proj_kernel(x_ref, o_ref, w_ref, h_ref):
    o = jnp.concatenate([o_ref[h] for h in range(N_HEADS)], axis=-1)
    h_ref[...] = x_ref[...] + jnp.dot(o, w_ref[...], preferred_element_type=F32)


def _out_proj(x, o, w, *, seq):
    t = x.shape[0]
    tiles_per_seq = seq // ROW_TILE
    row = lambda i: (i, 0)
    return pl.pallas_call(
        _out_proj_kernel,
        grid=(t // ROW_TILE,),
        in_specs=[
            pl.BlockSpec((ROW_TILE, D_MODEL), row),
            pl.BlockSpec((None, N_HEADS, ROW_TILE, V_HEAD),
                         lambda i: (i // tiles_per_seq, 0, i % tiles_per_seq, 0)),
            pl.BlockSpec(w.shape, lambda i: (0, 0)),
        ],
        out_specs=pl.BlockSpec((ROW_TILE, D_MODEL), row),
        out_shape=jax.ShapeDtypeStruct((t, D_MODEL), F32),
        compiler_params=_params("parallel"),
        name="out_proj",
    )(x, o, w)


def _mlp_kernel(x_ref, g_ref, wu_ref, wd_ref, fg_ref, o_ref, xn_sc, *, final_norm):
    j = pl.program_id(1)

    @pl.when(j == 0)
    def _():
        x = x_ref[...]
        xn_sc[...] = _rms(x, g_ref[...]).astype(BF16)
        o_ref[...] = x

    for c in range(MLP_FF_TILE // MLP_FF_SUB):
        cs = slice(c * MLP_FF_SUB, (c + 1) * MLP_FF_SUB)
        u = jnp.dot(xn_sc[...], wu_ref[:, cs], preferred_element_type=F32)
        u = jnp.maximum(u, 0.0)
        o_ref[...] += jnp.dot((u * u).astype(BF16), wd_ref[cs, :], preferred_element_type=F32)

    if final_norm:
        @pl.when(j == pl.num_programs(1) - 1)
        def _():
            o_ref[...] = _rms(o_ref[...], fg_ref[...])


def _mlp(x, g, wu, wd, fg, *, layer, final_norm):
    t = x.shape[0]
    row = lambda i, j: (i, 0)
    fixed = lambda i, j: (0, 0)
    return pl.pallas_call(
        functools.partial(_mlp_kernel, final_norm=final_norm),
        grid=(t // MLP_ROW_TILE, D_FF // MLP_FF_TILE),
        in_specs=[
            pl.BlockSpec((MLP_ROW_TILE, D_MODEL), row),
            pl.BlockSpec((1, D_MODEL), fixed),
            pl.BlockSpec((None, D_MODEL, MLP_FF_TILE), lambda i, j: (layer, 0, j)),
            pl.BlockSpec((None, MLP_FF_TILE, D_MODEL), lambda i, j: (layer, j, 0)),
            pl.BlockSpec((1, D_MODEL), fixed),
        ],
        out_specs=pl.BlockSpec((MLP_ROW_TILE, D_MODEL), row),
        out_shape=jax.ShapeDtypeStruct((t, D_MODEL), F32),
        scratch_shapes=[pltpu.VMEM((MLP_ROW_TILE, D_MODEL), BF16)],
        compiler_params=_params("parallel", "arbitrary"),
        name="mlp_final" if final_norm else "mlp",
    )(x, g, wu, wd, fg)


def _pool_mix_kernel(x_ref, halo_ref, g_ref, w_ref, ps_ref, o_ref, buf_sc, *, tiles_per_seq):
    tile = POOL_ROW_TILE
    i = pl.program_id(0)
    x = x_ref[...]
    g = g_ref[...]
    xn = _rms(x, g)
    t_in_seq = i % tiles_per_seq
    halo = jnp.where(t_in_seq == 0, 0.0, _rms(halo_ref[...], g))
    buf_sc[:POOL_HALO, :] = halo
    buf_sc[POOL_HALO:, :] = xn
    t = t_in_seq * tile + lax.broadcasted_iota(jnp.int32, (tile, 1), 0)
    for gi, win in enumerate(POOL_WINDOWS):
        cols = slice(gi * POOL_GROUP, (gi + 1) * POOL_GROUP)
        xg = xn[:, cols]
        acc = buf_sc[:, cols]
        span = 1
        while span < win:
            acc = acc + pltpu.roll(acc, span, 0)
            span *= 2
        acc = acc[POOL_HALO:, :]
        cnt = jnp.minimum(t + 1, win).astype(F32)
        y = (acc / cnt - xg).astype(BF16)
        mix = jnp.dot(y, w_ref[gi], preferred_element_type=F32)
        o_ref[:, cols] = x[:, cols] + mix * ps_ref[:, cols]


def _pool_mix(x, g, w, ps, *, seq):
    t = x.shape[0]
    tiles_per_seq = seq // POOL_ROW_TILE
    halo_blocks = POOL_ROW_TILE // POOL_HALO
    row = lambda i: (i, 0)
    fixed = lambda i: (0, 0)
    return pl.pallas_call(
        functools.partial(_pool_mix_kernel, tiles_per_seq=tiles_per_seq),
        grid=(t // POOL_ROW_TILE,),
        in_specs=[
            pl.BlockSpec((POOL_ROW_TILE, D_MODEL), row),
            pl.BlockSpec((POOL_HALO, D_MODEL), lambda i: (jnp.maximum(i * halo_blocks - 1, 0), 0)),
            pl.BlockSpec((1, D_MODEL), fixed),
            pl.BlockSpec(w.shape, lambda i: (0, 0, 0)),
            pl.BlockSpec((1, D_MODEL), fixed),
        ],
        out_specs=pl.BlockSpec((POOL_ROW_TILE, D_MODEL), row),
        out_shape=jax.ShapeDtypeStruct((t, D_MODEL), F32),
        scratch_shapes=[pltpu.VMEM((POOL_ROW_TILE + POOL_HALO, D_MODEL), F32)],
        compiler_params=_params("parallel"),
        name="pool_mix",
    )(x, x, g, w, ps)


def kernel(x, positions, ln_mix, ln_mlp, w_dqkv, q_norm, kv_norm, w_uq, w_ukv, w_o,
           w_pool, pool_scale, w_up, w_down, final_norm):
    batch, seq, d = x.shape
    assert d == D_MODEL and all(
        seq % tile == 0 for tile in (ATTN_TILE, ROW_TILE, DOWN_ROW_TILE, POOL_ROW_TILE))
    t = batch * seq
    h = x.reshape(t, d)

    w_dqkv_p = jnp.pad(w_dqkv[0], ((0, 0), (0, LANES - QK_ROPE))).astype(BF16)
    wqt = w_uq[0].T.astype(BF16)
    wkv = w_ukv[0].reshape(KV_LORA, N_HEADS, QK_NOPE + V_HEAD)
    wk = wkv[:, :, :QK_NOPE].reshape(KV_LORA, N_HEADS * QK_NOPE).astype(BF16)
    wvt = wkv[:, :, QK_NOPE:].reshape(KV_LORA, N_HEADS * V_HEAD).T.astype(BF16)
    wp = w_pool[0].astype(BF16)

    inv_freq = 1.0 / (ROPE_THETA ** (jnp.arange(0, QK_ROPE, 2, dtype=F32) / QK_ROPE))
    pos_rows = positions.reshape(t // ATTN_TILE, 1, ATTN_TILE)
    invf_col = inv_freq.reshape(QK_ROPE // 2, 1)
    scale = math.log2(math.e) / math.sqrt(QK_HEAD)

    row = lambda v: v.reshape(1, -1)

    cq, ckv, kr = _attn_down(h, row(ln_mix[0]), w_dqkv_p, row(q_norm[0]), row(kv_norm[0]),
                             positions, invf_col)
    qt, k, vt, wd = _attn_up(cq, ckv, kr, pos_rows, invf_col, wqt, wk, wvt, w_down,
                             batch=batch, seq=seq, scale=scale)
    o, (wu, wo) = _flash(qt, k, vt, (w_up, w_o[0]), batch=batch, seq=seq)
    h = _out_proj(h, o, wo, seq=seq)
    h = _mlp(h, row(ln_mlp[0]), wu, wd, row(final_norm), layer=0, final_norm=False)

    h = _pool_mix(h, row(ln_mix[1]), wp, row(pool_scale[0]), seq=seq)
    h = _mlp(h, row(ln_mlp[1]), wu, wd, row(final_norm), layer=1, final_norm=True)
    return h.reshape(batch, seq, d)
```

```python
import functools
import math

import jax
import jax.numpy as jnp
from jax import lax
from jax.experimental import pallas as pl
from jax.experimental.pallas import tpu as pltpu

D_MODEL = 2048
N_HEADS = 16
Q_LORA = 512
KV_LORA = 512
QK_NOPE = 128
QK_ROPE = 64
V_HEAD = 128
QK_HEAD = QK_NOPE + QK_ROPE
ROPE_THETA = 10000.0
POOL_WINDOWS = (2, 4, 8, 16)
POOL_GROUP = D_MODEL // len(POOL_WINDOWS)
D_FF = 4 * D_MODEL
EPS = 1e-6

LANES = 128
BF16_ROWS = 16
QK_PAD = 2 * LANES
V_ROWS = V_HEAD + 16
POOL_HALO = 16
VMEM_LIMIT_BYTES = 56 * 1024 * 1024

F32 = jnp.float32
BF16 = jnp.bfloat16

ROW_TILE = 512
DOWN_ROW_TILE = 1024
DOWN_SUB_ROWS = 128
POOL_ROW_TILE = 1024
ATTN_TILE = 512
FLASH_HEADS = 2
FLASH_UNROLL = 14
MLP_ROW_TILE = 512
MLP_FF_TILE = 2048
MLP_FF_SUB = 1024


def _params(*semantics):
    return pltpu.CompilerParams(dimension_semantics=semantics,
                                vmem_limit_bytes=VMEM_LIMIT_BYTES)


def _rms(x, g):
    ms = jnp.mean(x * x, axis=-1, keepdims=True)
    return x * lax.rsqrt(ms + EPS) * g


def _rope_tables(pos_row, invf_col):
    ang = invf_col * pos_row.astype(F32)
    cos = jnp.cos(ang)
    sin = jnp.sin(ang)
    z = jnp.zeros_like(cos)
    pad = jnp.zeros((LANES - QK_ROPE, ang.shape[1]), F32)
    cos_t = jnp.concatenate([cos, cos, pad], axis=0)
    sin_lo_t = jnp.concatenate([-sin, z, pad], axis=0)
    sin_hi_t = jnp.concatenate([z, sin, pad], axis=0)
    return cos_t.T, sin_lo_t.T, sin_hi_t.T


def _rope(x, cos, sin_lo, sin_hi):
    half = QK_ROPE // 2
    return (x * cos + pltpu.roll(x, LANES - half, 1) * sin_lo
            + pltpu.roll(x, half, 1) * sin_hi)


def _attn_down_kernel(x_ref, g_ref, w_ref, qn_ref, kvn_ref, pos_ref, invf_ref, wc_ref,
                      cq_ref, ckv_ref, kr_ref, wc_bf_ref):
    wc_bf_ref[...] = wc_ref[...].astype(BF16)
    cos, sin_lo, sin_hi = _rope_tables(pos_ref[...], invf_ref[...])
    for r in range(DOWN_ROW_TILE // DOWN_SUB_ROWS):
        rs = slice(r * DOWN_SUB_ROWS, (r + 1) * DOWN_SUB_ROWS)
        xn = _rms(x_ref[rs, :], g_ref[...]).astype(BF16)
        c = jnp.dot(xn, w_ref[...], preferred_element_type=F32)
        cq_ref[rs, :] = _rms(c[:, :Q_LORA], qn_ref[...]).astype(BF16)
        ckv_ref[rs, :] = _rms(c[:, Q_LORA:Q_LORA + KV_LORA], kvn_ref[...]).astype(BF16)
        kr_ref[rs, :] = _rope(c[:, Q_LORA + KV_LORA:], cos[rs, :], sin_lo[rs, :],
                              sin_hi[rs, :]).astype(BF16)


def _attn_down(x, g, w, qn, kvn, positions, invf_col, w_cast):
    t = x.shape[0]
    n = w.shape[1]
    steps = t // DOWN_ROW_TILE
    row = lambda i: (i, 0)
    fixed = lambda i: (0, 0)
    pos_rows = positions.reshape(steps, 1, DOWN_ROW_TILE)
    assert w_cast.shape[0] % (steps * BF16_ROWS) == 0
    slab = (w_cast.shape[0] // steps, w_cast.shape[1])
    return pl.pallas_call(
        _attn_down_kernel,
        grid=(steps,),
        in_specs=[
            pl.BlockSpec((DOWN_ROW_TILE, D_MODEL), row),
            pl.BlockSpec((1, D_MODEL), fixed),
            pl.BlockSpec((D_MODEL, n), fixed),
            pl.BlockSpec((1, Q_LORA), fixed),
            pl.BlockSpec((1, KV_LORA), fixed),
            pl.BlockSpec((None, 1, DOWN_ROW_TILE), lambda i: (i, 0, 0)),
            pl.BlockSpec(invf_col.shape, fixed),
            pl.BlockSpec(slab, row),
        ],
        out_specs=[
            pl.BlockSpec((DOWN_ROW_TILE, Q_LORA), row),
            pl.BlockSpec((DOWN_ROW_TILE, KV_LORA), row),
            pl.BlockSpec((DOWN_ROW_TILE, LANES), row),
            pl.BlockSpec(slab, row),
        ],
        out_shape=[
            jax.ShapeDtypeStruct((t, Q_LORA), BF16),
            jax.ShapeDtypeStruct((t, KV_LORA), BF16),
            jax.ShapeDtypeStruct((t, LANES), BF16),
            jax.ShapeDtypeStruct(w_cast.shape, BF16),
        ],
        compiler_params=_params("parallel"),
        name="attn_down",
    )(x, g, w, qn, kvn, pos_rows, invf_col, w_cast)


def _attn_up_kernel(cq_ref, ckv_ref, kr_ref, pos_ref, invf_ref, wqt_ref, wk_ref, wvt_ref,
                    qt_ref, k_ref, vt_ref, *, scale):
    cq = cq_ref[...]
    ckv = ckv_ref[...]
    kr = kr_ref[...]
    half = QK_ROPE // 2
    ang = invf_ref[...] * pos_ref[...].astype(F32)
    cos = jnp.cos(ang)
    sin = jnp.sin(ang)
    zeros = jnp.zeros((QK_PAD - QK_HEAD, ATTN_TILE), BF16)
    group = 4
    for c in range(N_HEADS // group):
        qt = lax.dot_general(wqt_ref[c * group * QK_HEAD:(c + 1) * group * QK_HEAD, :], cq,
                             (((1,), (1,)), ((), ())), preferred_element_type=F32)
        for hh in range(group):
            h = c * group + hh
            qh = qt[hh * QK_HEAD:(hh + 1) * QK_HEAD, :]
            x1 = qh[QK_NOPE:QK_NOPE + half, :]
            x2 = qh[QK_NOPE + half:, :]
            qt_ref[h, :QK_NOPE, :] = (qh[:QK_NOPE, :] * scale).astype(BF16)
            qt_ref[h, QK_NOPE:QK_NOPE + half, :] = ((x1 * cos - x2 * sin) * scale).astype(BF16)
            qt_ref[h, QK_NOPE + half:QK_HEAD, :] = ((x2 * cos + x1 * sin) * scale).astype(BF16)
            qt_ref[h, QK_HEAD:, :] = zeros
            k_ref[h, :, QK_NOPE:] = kr
    for p in range(N_HEADS // 2):
        kk = jnp.dot(ckv, wk_ref[:, p * QK_PAD:(p + 1) * QK_PAD], preferred_element_type=F32)
        k_ref[2 * p, :, :QK_NOPE] = kk[:, :QK_NOPE].astype(BF16)
        k_ref[2 * p + 1, :, :QK_NOPE] = kk[:, QK_NOPE:].astype(BF16)
    pad_rows = V_ROWS - V_HEAD
    ones_row = (lax.broadcasted_iota(jnp.int32, (pad_rows, ATTN_TILE), 0) == 0).astype(BF16)
    group = 4
    for c in range(N_HEADS // group):
        vt = lax.dot_general(wvt_ref[c * group * V_HEAD:(c + 1) * group * V_HEAD, :], ckv,
                             (((1,), (1,)), ((), ())), preferred_element_type=F32)
        for hh in range(group):
            vt_ref[c * group + hh, :V_HEAD, :] = vt[hh * V_HEAD:(hh + 1) * V_HEAD, :].astype(BF16)
            vt_ref[c * group + hh, V_HEAD:, :] = ones_row


def _attn_up(cq, ckv, kr, pos_rows, invf_col, wqt, wk, wvt, *, batch, seq, scale):
    t = cq.shape[0]
    tiles_per_seq = seq // ATTN_TILE
    row = lambda i: (i, 0)
    fixed = lambda i: (0, 0)
    head_major = lambda i: (i // tiles_per_seq, 0, i % tiles_per_seq, 0)
    chunk_major = lambda i: (i // tiles_per_seq, i % tiles_per_seq, 0, 0, 0)
    return pl.pallas_call(
        functools.partial(_attn_up_kernel, scale=scale),
        grid=(t // ATTN_TILE,),
        in_specs=[
            pl.BlockSpec((ATTN_TILE, Q_LORA), row),
            pl.BlockSpec((ATTN_TILE, KV_LORA), row),
            pl.BlockSpec((ATTN_TILE, LANES), row),
            pl.BlockSpec((None, 1, ATTN_TILE), lambda i: (i, 0, 0)),
            pl.BlockSpec(invf_col.shape, fixed),
            pl.BlockSpec(wqt.shape, fixed),
            pl.BlockSpec(wk.shape, fixed),
            pl.BlockSpec(wvt.shape, fixed),
        ],
        out_specs=[
            pl.BlockSpec((None, None, N_HEADS, QK_PAD, ATTN_TILE), chunk_major),
            pl.BlockSpec((None, N_HEADS, ATTN_TILE, QK_PAD), head_major),
            pl.BlockSpec((None, None, N_HEADS, V_ROWS, ATTN_TILE), chunk_major),
        ],
        out_shape=[
            jax.ShapeDtypeStruct((batch, tiles_per_seq, N_HEADS, QK_PAD, ATTN_TILE), BF16),
            jax.ShapeDtypeStruct((batch, N_HEADS, seq, QK_PAD), BF16),
            jax.ShapeDtypeStruct((batch, tiles_per_seq, N_HEADS, V_ROWS, ATTN_TILE), BF16),
        ],
        compiler_params=_params("parallel"),
        name="attn_up",
    )(cq, ckv, kr, pos_rows, invf_col, wqt, wk, wvt)


def _flash_kernel(qt_ref, k_ref, vt_ref, *rest, n_tiles, n_cast):
    w_refs = rest[:n_cast]
    o_ref = rest[n_cast]
    w_bf_refs = rest[n_cast + 1:2 * n_cast + 1]
    s_sc, cmax_sc, m_sc, acc_sc = rest[2 * n_cast + 1:]
    tile = ATTN_TILE
    half = tile // 2

    def rows(idx):
        if isinstance(idx, int):
            return pl.ds(idx * tile, tile)
        return pl.ds(pl.multiple_of(idx * tile, tile), tile)

    def produce(hh, i, j, slot):
        s = jnp.dot(k_ref[hh, rows(j), :], qt_ref[i, hh], preferred_element_type=F32)
        s_sc[hh, slot] = s
        cmax_sc[hh, slot] = jnp.max(s, axis=0, keepdims=True)

    def consume(hh, i, j, slot):
        m_prev = m_sc[hh, i]
        m_new = jnp.maximum(m_prev, cmax_sc[hh, slot])
        alpha = jnp.exp2(m_prev - m_new)
        p = jnp.exp2(s_sc[hh, slot] - m_new).astype(BF16)
        m_sc[hh, i] = m_new
        acc_sc[hh, i] = alpha * acc_sc[hh, i] + jnp.dot(vt_ref[j, hh], p,
                                                        preferred_element_type=F32)

    def produce_diagonal(hh, i, slot):
        s_lo = jnp.dot(k_ref[hh, pl.ds(i * tile, half), :], qt_ref[i, hh],
                       preferred_element_type=F32)
        s_hi = jnp.dot(k_ref[hh, pl.ds(i * tile + half, half), :], qt_ref[i, hh, :, half:],
                       preferred_element_type=F32)
        kpos = lax.broadcasted_iota(jnp.int32, (half, half), 0)
        qpos = lax.broadcasted_iota(jnp.int32, (half, half), 1)
        causal = kpos <= qpos
        s_ll = jnp.where(causal, s_lo[:, :half], -jnp.inf)
        s_hh = jnp.where(causal, s_hi, -jnp.inf)
        s_sc[hh, slot, :half, :half] = s_ll
        s_sc[hh, slot, :half, half:] = s_lo[:, half:]
        s_sc[hh, slot, half:, half:] = s_hh
        cmax_sc[hh, slot, :, :half] = jnp.max(s_ll, axis=0, keepdims=True)
        cmax_sc[hh, slot, :, half:] = jnp.maximum(
            jnp.max(s_lo[:, half:], axis=0, keepdims=True), jnp.max(s_hh, axis=0, keepdims=True))

    def consume_diagonal(hh, i, slot):
        m = cmax_sc[hh, slot]
        p_lo = jnp.exp2(s_sc[hh, slot, :half, :] - m).astype(BF16)
        p_hi = jnp.exp2(s_sc[hh, slot, half:, half:] - m[:, half:]).astype(BF16)
        pv = jnp.dot(vt_ref[i, hh, :, :half], p_lo, preferred_element_type=F32)
        pv_hi = jnp.dot(vt_ref[i, hh, :, half:], p_hi, preferred_element_type=F32)
        m_sc[hh, i] = m
        acc_sc[hh, i, :, :half] = pv[:, :half]
        acc_sc[hh, i, :, half:] = pv[:, half:] + pv_hi

    assert n_tiles % 2 == 0
    heads = range(FLASH_HEADS)
    for hh in heads:
        produce_diagonal(hh, 0, 0)
    for i in range(n_tiles):
        for hh in heads:
            if i + 1 < n_tiles:
                produce_diagonal(hh, i + 1, (i + 1) % 2)
            else:
                produce(hh, 1, 0, (i + 1) % 2)
        for hh in heads:
            consume_diagonal(hh, i, i % 2)

    def succ(i, j):
        wrap = j + 1 >= i
        return jnp.where(wrap, i + 1, i), jnp.where(wrap, 0, j + 1)

    n_off = n_tiles * (n_tiles - 1) // 2
    n_trips = n_off // FLASH_UNROLL
    assert n_off % FLASH_UNROLL == 0 and FLASH_UNROLL % 2 == 0

    def body(trip, cur):
        for w_ref, w_bf_ref in zip(w_refs, w_bf_refs):
            n_rows = w_ref.shape[0] // n_trips
            assert n_rows % BF16_ROWS == 0
            rs = pl.ds(pl.multiple_of(trip * n_rows, n_rows), n_rows)
            w_bf_ref[rs, :] = w_ref[rs, :].astype(BF16)
        for u in range(FLASH_UNROLL):
            nxt = succ(*cur)
            for hh in heads:
                produce(hh, jnp.minimum(nxt[0], n_tiles - 1), jnp.minimum(nxt[1], n_tiles - 2),
                        1 - u % 2)
            for hh in heads:
                consume(hh, *cur, u % 2)
            cur = nxt
        return cur

    lax.fori_loop(0, n_trips, body, (jnp.int32(1), jnp.int32(0)))

    for hh in heads:
        for i in range(n_tiles):
            acc = acc_sc[hh, i]
            o = acc[:V_HEAD, :] * (1.0 / acc[V_HEAD:V_HEAD + 1, :])
            o_ref[hh, i * tile:(i + 1) * tile, :] = o.T.astype(BF16)


def _flash(q, k, vt, weights, *, batch, seq):
    n_tiles = seq // ATTN_TILE
    head_groups = N_HEADS // FLASH_HEADS
    steps = batch * head_groups
    w2d = [w.reshape(-1, w.shape[-1]) for w in weights]
    slabs = [(w.shape[0] // steps, w.shape[1]) for w in w2d]
    assert all(w.shape[0] % steps == 0 for w in w2d)
    step_row = lambda b, g: (b * head_groups + g, 0)
    heads_of = lambda b, g: (b, g, 0, 0)
    chunked_heads_of = lambda b, g: (b, 0, g, 0, 0)
    o, *w_bf = pl.pallas_call(
        functools.partial(_flash_kernel, n_tiles=n_tiles, n_cast=len(w2d)),
        grid=(batch, head_groups),
        in_specs=[
            pl.BlockSpec((None, n_tiles, FLASH_HEADS, QK_PAD, ATTN_TILE), chunked_heads_of),
            pl.BlockSpec((None, FLASH_HEADS, seq, QK_PAD), heads_of),
            pl.BlockSpec((None, n_tiles, FLASH_HEADS, V_ROWS, ATTN_TILE), chunked_heads_of),
        ] + [pl.BlockSpec(slab, step_row) for slab in slabs],
        out_specs=[
            pl.BlockSpec((None, FLASH_HEADS, seq, V_HEAD), heads_of),
        ] + [pl.BlockSpec(slab, step_row) for slab in slabs],
        out_shape=[
            jax.ShapeDtypeStruct((batch, N_HEADS, seq, V_HEAD), BF16),
        ] + [jax.ShapeDtypeStruct(w.shape, BF16) for w in w2d],
        scratch_shapes=[
            pltpu.VMEM((FLASH_HEADS, 2, ATTN_TILE, ATTN_TILE), F32),
            pltpu.VMEM((FLASH_HEADS, 2, 1, ATTN_TILE), F32),
            pltpu.VMEM((FLASH_HEADS, n_tiles, 1, ATTN_TILE), F32),
            pltpu.VMEM((FLASH_HEADS, n_tiles, V_ROWS, ATTN_TILE), F32),
        ],
        compiler_params=_params("parallel", "parallel"),
        name="flash_attn",
    )(q, k, vt, *w2d)
    return o, [wb.reshape(w.shape) for wb, w in zip(w_bf, weights)]


def _out_proj_kernel(x_ref, o_ref, w_ref, h_ref):
    o = jnp.concatenate([o_ref[h] for h in range(N_HEADS)], axis=-1)
    h_ref[...] = x_ref[...] + jnp.dot(o, w_ref[...], preferred_element_type=F32)


def _out_proj(x, o, w, *, seq):
    t = x.shape[0]
    tiles_per_seq = seq // ROW_TILE
    row = lambda i: (i, 0)
    return pl.pallas_call(
        _out_proj_kernel,
        grid=(t // ROW_TILE,),
        in_specs=[
            pl.BlockSpec((ROW_TILE, D_MODEL), row),
            pl.BlockSpec((None, N_HEADS, ROW_TILE, V_HEAD),
                         lambda i: (i // tiles_per_seq, 0, i % tiles_per_seq, 0)),
            pl.BlockSpec(w.shape, lambda i: (0, 0)),
        ],
        out_specs=pl.BlockSpec((ROW_TILE, D_MODEL), row),
        out_shape=jax.ShapeDtypeStruct((t, D_MODEL), F32),
        compiler_params=_params("parallel"),
        name="out_proj",
    )(x, o, w)


def _mlp_kernel(x_ref, g_ref, wu_ref, wd_ref, fg_ref, o_ref, xn_sc, *, final_norm):
    j = pl.program_id(1)

    @pl.when(j == 0)
    def _():
        x = x_ref[...]
        xn_sc[...] = _rms(x, g_ref[...]).astype(BF16)
        o_ref[...] = x

    for c in range(MLP_FF_TILE // MLP_FF_SUB):
        cs = slice(c * MLP_FF_SUB, (c + 1) * MLP_FF_SUB)
        u = jnp.dot(xn_sc[...], wu_ref[:, cs], preferred_element_type=F32)
        u = jnp.maximum(u, 0.0)
        o_ref[...] += jnp.dot((u * u).astype(BF16), wd_ref[cs, :], preferred_element_type=F32)

    if final_norm:
        @pl.when(j == pl.num_programs(1) - 1)
        def _():
            o_ref[...] = _rms(o_ref[...], fg_ref[...])


def _mlp(x, g, wu, wd, fg, *, wu_layer, wd_layer, final_norm):
    t = x.shape[0]
    row = lambda i, j: (i, 0)
    fixed = lambda i, j: (0, 0)
    return pl.pallas_call(
        functools.partial(_mlp_kernel, final_norm=final_norm),
        grid=(t // MLP_ROW_TILE, D_FF // MLP_FF_TILE),
        in_specs=[
            pl.BlockSpec((MLP_ROW_TILE, D_MODEL), row),
            pl.BlockSpec((1, D_MODEL), fixed),
            pl.BlockSpec((None, D_MODEL, MLP_FF_TILE), lambda i, j: (wu_layer, 0, j)),
            pl.BlockSpec((None, MLP_FF_TILE, D_MODEL), lambda i, j: (wd_layer, j, 0)),
            pl.BlockSpec((1, D_MODEL), fixed),
        ],
        out_specs=pl.BlockSpec((MLP_ROW_TILE, D_MODEL), row),
        out_shape=jax.ShapeDtypeStruct((t, D_MODEL), F32),
        scratch_shapes=[pltpu.VMEM((MLP_ROW_TILE, D_MODEL), BF16)],
        compiler_params=_params("parallel", "arbitrary"),
        name="mlp_final" if final_norm else "mlp",
    )(x, g, wu, wd, fg)


def _pool_mix_kernel(x_ref, halo_ref, g_ref, w_ref, ps_ref, o_ref, buf_sc, *, tiles_per_seq):
    tile = POOL_ROW_TILE
    i = pl.program_id(0)
    x = x_ref[...]
    g = g_ref[...]
    xn = _rms(x, g)
    t_in_seq = i % tiles_per_seq
    halo = jnp.where(t_in_seq == 0, 0.0, _rms(halo_ref[...], g))
    buf_sc[:POOL_HALO, :] = halo
    buf_sc[POOL_HALO:, :] = xn
    t = t_in_seq * tile + lax.broadcasted_iota(jnp.int32, (tile, 1), 0)
    for gi, win in enumerate(POOL_WINDOWS):
        cols = slice(gi * POOL_GROUP, (gi + 1) * POOL_GROUP)
        xg = xn[:, cols]
        acc = buf_sc[:, cols]
        span = 1
        while span < win:
            acc = acc + pltpu.roll(acc, span, 0)
            span *= 2
        acc = acc[POOL_HALO:, :]
        cnt = jnp.minimum(t + 1, win).astype(F32)
        y = (acc / cnt - xg).astype(BF16)
        mix = jnp.dot(y, w_ref[gi], preferred_element_type=F32)
        o_ref[:, cols] = x[:, cols] + mix * ps_ref[:, cols]


def _pool_mix(x, g, w, ps, *, seq):
    t = x.shape[0]
    tiles_per_seq = seq // POOL_ROW_TILE
    halo_blocks = POOL_ROW_TILE // POOL_HALO
    row = lambda i: (i, 0)
    fixed = lambda i: (0, 0)
    return pl.pallas_call(
        functools.partial(_pool_mix_kernel, tiles_per_seq=tiles_per_seq),
        grid=(t // POOL_ROW_TILE,),
        in_specs=[
            pl.BlockSpec((POOL_ROW_TILE, D_MODEL), row),
            pl.BlockSpec((POOL_HALO, D_MODEL), lambda i: (jnp.maximum(i * halo_blocks - 1, 0), 0)),
            pl.BlockSpec((1, D_MODEL), fixed),
            pl.BlockSpec(w.shape, lambda i: (0, 0, 0)),
            pl.BlockSpec((1, D_MODEL), fixed),
        ],
        out_specs=pl.BlockSpec((POOL_ROW_TILE, D_MODEL), row),
        out_shape=jax.ShapeDtypeStruct((t, D_MODEL), F32),
        scratch_shapes=[pltpu.VMEM((POOL_ROW_TILE + POOL_HALO, D_MODEL), F32)],
        compiler_params=_params("parallel"),
        name="pool_mix",
    )(x, x, g, w, ps)


def kernel(x, positions, ln_mix, ln_mlp, w_dqkv, q_norm, kv_norm, w_uq, w_ukv, w_o,
           w_pool, pool_scale, w_up, w_down, final_norm):
    batch, seq, d = x.shape
    assert d == D_MODEL and all(
        seq % tile == 0 for tile in (ATTN_TILE, ROW_TILE, DOWN_ROW_TILE, POOL_ROW_TILE))
    t = batch * seq
    h = x.reshape(t, d)

    w_dqkv_p = jnp.pad(w_dqkv[0], ((0, 0), (0, LANES - QK_ROPE))).astype(BF16)
    wqt = w_uq[0].T.astype(BF16)
    wkv = w_ukv[0].reshape(KV_LORA, N_HEADS, QK_NOPE + V_HEAD)
    wk = wkv[:, :, :QK_NOPE].reshape(KV_LORA, N_HEADS * QK_NOPE).astype(BF16)
    wvt = wkv[:, :, QK_NOPE:].reshape(KV_LORA, N_HEADS * V_HEAD).T.astype(BF16)
    wp = w_pool[0].astype(BF16)

    inv_freq = 1.0 / (ROPE_THETA ** (jnp.arange(0, QK_ROPE, 2, dtype=F32) / QK_ROPE))
    pos_rows = positions.reshape(t // ATTN_TILE, 1, ATTN_TILE)
    invf_col = inv_freq.reshape(QK_ROPE // 2, 1)
    scale = math.log2(math.e) / math.sqrt(QK_HEAD)

    row = lambda v: v.reshape(1, -1)

    cq, ckv, kr, wd1 = _attn_down(h, row(ln_mix[0]), w_dqkv_p, row(q_norm[0]), row(kv_norm[0]),
                                  positions, invf_col, w_down[1])
    qt, k, vt = _attn_up(cq, ckv, kr, pos_rows, invf_col, wqt, wk, wvt,
                         batch=batch, seq=seq, scale=scale)
    o, (wu, wd0, wo) = _flash(qt, k, vt, (w_up, w_down[0], w_o[0]), batch=batch, seq=seq)
    h = _out_proj(h, o, wo, seq=seq)
    h = _mlp(h, row(ln_mlp[0]), wu, wd0[None], row(final_norm),
             wu_layer=0, wd_layer=0, final_norm=False)

    h = _pool_mix(h, row(ln_mix[1]), wp, row(pool_scale[0]), seq=seq)
    h = _mlp(h, row(ln_mlp[1]), wu, wd1[None], row(final_norm),
             wu_layer=1, wd_layer=0, final_norm=True)
    return h.reshape(batch, seq, d)
```

```python
import functools
import math

import jax
import jax.numpy as jnp
from jax import lax
from jax.experimental import pallas as pl
from jax.experimental.pallas import tpu as pltpu

D_MODEL = 2048
N_HEADS = 16
Q_LORA = 512
KV_LORA = 512
QK_NOPE = 128
QK_ROPE = 64
V_HEAD = 128
QK_HEAD = QK_NOPE + QK_ROPE
ROPE_THETA = 10000.0
POOL_WINDOWS = (2, 4, 8, 16)
POOL_GROUP = D_MODEL // len(POOL_WINDOWS)
D_FF = 4 * D_MODEL
EPS = 1e-6

LANES = 128
BF16_ROWS = 16
QK_PAD = 2 * LANES
V_ROWS = V_HEAD + 16
POOL_HALO = 16
VMEM_LIMIT_BYTES = 56 * 1024 * 1024

F32 = jnp.float32
BF16 = jnp.bfloat16

ROW_TILE = 512
DOWN_ROW_TILE = 1024
DOWN_SUB_ROWS = 128
POOL_ROW_TILE = 1024
ATTN_TILE = 512
FLASH_HEADS = 2
FLASH_UNROLL = 14
MLP_ROW_TILE = 512
MLP_FF_TILE = 2048
MLP_FF_SUB = 1024


def _params(*semantics):
    return pltpu.CompilerParams(dimension_semantics=semantics,
                                vmem_limit_bytes=VMEM_LIMIT_BYTES)


def _rms(x, g):
    ms = jnp.mean(x * x, axis=-1, keepdims=True)
    return x * lax.rsqrt(ms + EPS) * g


def _rope_tables(pos_row, invf_col):
    ang = invf_col * pos_row.astype(F32)
    cos = jnp.cos(ang)
    sin = jnp.sin(ang)
    z = jnp.zeros_like(cos)
    pad = jnp.zeros((LANES - QK_ROPE, ang.shape[1]), F32)
    cos_t = jnp.concatenate([cos, cos, pad], axis=0)
    sin_lo_t = jnp.concatenate([-sin, z, pad], axis=0)
    sin_hi_t = jnp.concatenate([z, sin, pad], axis=0)
    return cos_t.T, sin_lo_t.T, sin_hi_t.T


def _rope(x, cos, sin_lo, sin_hi):
    half = QK_ROPE // 2
    return (x * cos + pltpu.roll(x, LANES - half, 1) * sin_lo
            + pltpu.roll(x, half, 1) * sin_hi)


def _attn_down_kernel(x_ref, g_ref, w_ref, qn_ref, kvn_ref, pos_ref, invf_ref, wc_ref,
                      cq_ref, ckv_ref, kr_ref, wc_bf_ref):
    wc_bf_ref[...] = wc_ref[...].astype(BF16)
    cos, sin_lo, sin_hi = _rope_tables(pos_ref[...], invf_ref[...])
    for r in range(DOWN_ROW_TILE // DOWN_SUB_ROWS):
        rs = slice(r * DOWN_SUB_ROWS, (r + 1) * DOWN_SUB_ROWS)
        xn = _rms(x_ref[rs, :], g_ref[...]).astype(BF16)
        c = jnp.dot(xn, w_ref[...], preferred_element_type=F32)
        cq_ref[rs, :] = _rms(c[:, :Q_LORA], qn_ref[...]).astype(BF16)
        ckv_ref[rs, :] = _rms(c[:, Q_LORA:Q_LORA + KV_LORA], kvn_ref[...]).astype(BF16)
        kr_ref[rs, :] = _rope(c[:, Q_LORA + KV_LORA:], cos[rs, :], sin_lo[rs, :],
                              sin_hi[rs, :]).astype(BF16)


def _attn_down(x, g, w, qn, kvn, positions, invf_col, w_cast, cast_first_row, cast_rows):
    t = x.shape[0]
    n = w.shape[1]
    steps = t // DOWN_ROW_TILE
    row = lambda i: (i, 0)
    fixed = lambda i: (0, 0)
    pos_rows = positions.reshape(steps, 1, DOWN_ROW_TILE)
    slab = (cast_rows // steps, w_cast.shape[1])
    assert cast_rows % (steps * BF16_ROWS) == 0 and cast_first_row % slab[0] == 0
    slab_offset = cast_first_row // slab[0]
    return pl.pallas_call(
        _attn_down_kernel,
        grid=(steps,),
        in_specs=[
            pl.BlockSpec((DOWN_ROW_TILE, D_MODEL), row),
            pl.BlockSpec((1, D_MODEL), fixed),
            pl.BlockSpec((D_MODEL, n), fixed),
            pl.BlockSpec((1, Q_LORA), fixed),
            pl.BlockSpec((1, KV_LORA), fixed),
            pl.BlockSpec((None, 1, DOWN_ROW_TILE), lambda i: (i, 0, 0)),
            pl.BlockSpec(invf_col.shape, fixed),
            pl.BlockSpec(slab, lambda i: (slab_offset + i, 0)),
        ],
        out_specs=[
            pl.BlockSpec((DOWN_ROW_TILE, Q_LORA), row),
            pl.BlockSpec((DOWN_ROW_TILE, KV_LORA), row),
            pl.BlockSpec((DOWN_ROW_TILE, LANES), row),
            pl.BlockSpec(slab, row),
        ],
        out_shape=[
            jax.ShapeDtypeStruct((t, Q_LORA), BF16),
            jax.ShapeDtypeStruct((t, KV_LORA), BF16),
            jax.ShapeDtypeStruct((t, LANES), BF16),
            jax.ShapeDtypeStruct((cast_rows, w_cast.shape[1]), BF16),
        ],
        compiler_params=_params("parallel"),
        name="attn_down",
    )(x, g, w, qn, kvn, pos_rows, invf_col, w_cast)


def _attn_up_kernel(cq_ref, ckv_ref, kr_ref, pos_ref, invf_ref, wqt_ref, wk_ref, wvt_ref,
                    qt_ref, k_ref, vt_ref, *, scale):
    cq = cq_ref[...]
    ckv = ckv_ref[...]
    kr = kr_ref[...]
    half = QK_ROPE // 2
    ang = invf_ref[...] * pos_ref[...].astype(F32)
    cos = jnp.cos(ang)
    sin = jnp.sin(ang)
    zeros = jnp.zeros((QK_PAD - QK_HEAD, ATTN_TILE), BF16)
    group = 4
    for c in range(N_HEADS // group):
        qt = lax.dot_general(wqt_ref[c * group * QK_HEAD:(c + 1) * group * QK_HEAD, :], cq,
                             (((1,), (1,)), ((), ())), preferred_element_type=F32)
        for hh in range(group):
            h = c * group + hh
            qh = qt[hh * QK_HEAD:(hh + 1) * QK_HEAD, :]
            x1 = qh[QK_NOPE:QK_NOPE + half, :]
            x2 = qh[QK_NOPE + half:, :]
            qt_ref[h, :QK_NOPE, :] = (qh[:QK_NOPE, :] * scale).astype(BF16)
            qt_ref[h, QK_NOPE:QK_NOPE + half, :] = ((x1 * cos - x2 * sin) * scale).astype(BF16)
            qt_ref[h, QK_NOPE + half:QK_HEAD, :] = ((x2 * cos + x1 * sin) * scale).astype(BF16)
            qt_ref[h, QK_HEAD:, :] = zeros
            k_ref[h, :, QK_NOPE:] = kr
    for p in range(N_HEADS // 2):
        kk = jnp.dot(ckv, wk_ref[:, p * QK_PAD:(p + 1) * QK_PAD], preferred_element_type=F32)
        k_ref[2 * p, :, :QK_NOPE] = kk[:, :QK_NOPE].astype(BF16)
        k_ref[2 * p + 1, :, :QK_NOPE] = kk[:, QK_NOPE:].astype(BF16)
    pad_rows = V_ROWS - V_HEAD
    ones_row = (lax.broadcasted_iota(jnp.int32, (pad_rows, ATTN_TILE), 0) == 0).astype(BF16)
    group = 4
    for c in range(N_HEADS // group):
        vt = lax.dot_general(wvt_ref[c * group * V_HEAD:(c + 1) * group * V_HEAD, :], ckv,
                             (((1,), (1,)), ((), ())), preferred_element_type=F32)
        for hh in range(group):
            vt_ref[c * group + hh, :V_HEAD, :] = vt[hh * V_HEAD:(hh + 1) * V_HEAD, :].astype(BF16)
            vt_ref[c * group + hh, V_HEAD:, :] = ones_row


def _attn_up(cq, ckv, kr, pos_rows, invf_col, wqt, wk, wvt, *, batch, seq, scale):
    t = cq.shape[0]
    tiles_per_seq = seq // ATTN_TILE
    row = lambda i: (i, 0)
    fixed = lambda i: (0, 0)
    head_major = lambda i: (i // tiles_per_seq, 0, i % tiles_per_seq, 0)
    chunk_major = lambda i: (i // tiles_per_seq, i % tiles_per_seq, 0, 0, 0)
    return pl.pallas_call(
        functools.partial(_attn_up_kernel, scale=scale),
        grid=(t // ATTN_TILE,),
        in_specs=[
            pl.BlockSpec((ATTN_TILE, Q_LORA), row),
            pl.BlockSpec((ATTN_TILE, KV_LORA), row),
            pl.BlockSpec((ATTN_TILE, LANES), row),
            pl.BlockSpec((None, 1, ATTN_TILE), lambda i: (i, 0, 0)),
            pl.BlockSpec(invf_col.shape, fixed),
            pl.BlockSpec(wqt.shape, fixed),
            pl.BlockSpec(wk.shape, fixed),
            pl.BlockSpec(wvt.shape, fixed),
        ],
        out_specs=[
            pl.BlockSpec((None, None, N_HEADS, QK_PAD, ATTN_TILE), chunk_major),
            pl.BlockSpec((None, N_HEADS, ATTN_TILE, QK_PAD), head_major),
            pl.BlockSpec((None, None, N_HEADS, V_ROWS, ATTN_TILE), chunk_major),
        ],
        out_shape=[
            jax.ShapeDtypeStruct((batch, tiles_per_seq, N_HEADS, QK_PAD, ATTN_TILE), BF16),
            jax.ShapeDtypeStruct((batch, N_HEADS, seq, QK_PAD), BF16),
            jax.ShapeDtypeStruct((batch, tiles_per_seq, N_HEADS, V_ROWS, ATTN_TILE), BF16),
        ],
        compiler_params=_params("parallel"),
        name="attn_up",
    )(cq, ckv, kr, pos_rows, invf_col, wqt, wk, wvt)


def _flash_kernel(qt_ref, k_ref, vt_ref, *rest, n_tiles, n_cast):
    w_refs = rest[:n_cast]
    o_ref = rest[n_cast]
    w_bf_refs = rest[n_cast + 1:2 * n_cast + 1]
    s_sc, cmax_sc, m_sc, acc_sc = rest[2 * n_cast + 1:]
    tile = ATTN_TILE
    half = tile // 2

    def rows(idx):
        if isinstance(idx, int):
            return pl.ds(idx * tile, tile)
        return pl.ds(pl.multiple_of(idx * tile, tile), tile)

    def produce(hh, i, j, slot):
        s = jnp.dot(k_ref[hh, rows(j), :], qt_ref[i, hh], preferred_element_type=F32)
        s_sc[hh, slot] = s
        cmax_sc[hh, slot] = jnp.max(s, axis=0, keepdims=True)

    def consume(hh, i, j, slot):
        m_prev = m_sc[hh, i]
        m_new = jnp.maximum(m_prev, cmax_sc[hh, slot])
        alpha = jnp.exp2(m_prev - m_new)
        p = jnp.exp2(s_sc[hh, slot] - m_new).astype(BF16)
        m_sc[hh, i] = m_new
        acc_sc[hh, i] = alpha * acc_sc[hh, i] + jnp.dot(vt_ref[j, hh], p,
                                                        preferred_element_type=F32)

    def produce_diagonal(hh, i, slot):
        s_lo = jnp.dot(k_ref[hh, pl.ds(i * tile, half), :], qt_ref[i, hh],
                       preferred_element_type=F32)
        s_hi = jnp.dot(k_ref[hh, pl.ds(i * tile + half, half), :], qt_ref[i, hh, :, half:],
                       preferred_element_type=F32)
        kpos = lax.broadcasted_iota(jnp.int32, (half, half), 0)
        qpos = lax.broadcasted_iota(jnp.int32, (half, half), 1)
        causal = kpos <= qpos
        s_ll = jnp.where(causal, s_lo[:, :half], -jnp.inf)
        s_hh = jnp.where(causal, s_hi, -jnp.inf)
        s_sc[hh, slot, :half, :half] = s_ll
        s_sc[hh, slot, :half, half:] = s_lo[:, half:]
        s_sc[hh, slot, half:, half:] = s_hh
        cmax_sc[hh, slot, :, :half] = jnp.max(s_ll, axis=0, keepdims=True)
        cmax_sc[hh, slot, :, half:] = jnp.maximum(
            jnp.max(s_lo[:, half:], axis=0, keepdims=True), jnp.max(s_hh, axis=0, keepdims=True))

    def consume_diagonal(hh, i, slot):
        m = cmax_sc[hh, slot]
        p_lo = jnp.exp2(s_sc[hh, slot, :half, :] - m).astype(BF16)
        p_hi = jnp.exp2(s_sc[hh, slot, half:, half:] - m[:, half:]).astype(BF16)
        pv = jnp.dot(vt_ref[i, hh, :, :half], p_lo, preferred_element_type=F32)
        pv_hi = jnp.dot(vt_ref[i, hh, :, half:], p_hi, preferred_element_type=F32)
        m_sc[hh, i] = m
        acc_sc[hh, i, :, :half] = pv[:, :half]
        acc_sc[hh, i, :, half:] = pv[:, half:] + pv_hi

    assert n_tiles % 2 == 0
    heads = range(FLASH_HEADS)
    for hh in heads:
        produce_diagonal(hh, 0, 0)
    for i in range(n_tiles):
        for hh in heads:
            if i + 1 < n_tiles:
                produce_diagonal(hh, i + 1, (i + 1) % 2)
            else:
                produce(hh, 1, 0, (i + 1) % 2)
        for hh in heads:
            consume_diagonal(hh, i, i % 2)

    def succ(i, j):
        wrap = j + 1 >= i
        return jnp.where(wrap, i + 1, i), jnp.where(wrap, 0, j + 1)

    n_off = n_tiles * (n_tiles - 1) // 2
    n_trips = n_off // FLASH_UNROLL
    assert n_off % FLASH_UNROLL == 0 and FLASH_UNROLL % 2 == 0

    def body(trip, cur):
        for w_ref, w_bf_ref in zip(w_refs, w_bf_refs):
            n_rows = w_ref.shape[0] // n_trips
            assert n_rows % BF16_ROWS == 0
            rs = pl.ds(pl.multiple_of(trip * n_rows, n_rows), n_rows)
            w_bf_ref[rs, :] = w_ref[rs, :].astype(BF16)
        for u in range(FLASH_UNROLL):
            nxt = succ(*cur)
            for hh in heads:
                produce(hh, jnp.minimum(nxt[0], n_tiles - 1), jnp.minimum(nxt[1], n_tiles - 2),
                        1 - u % 2)
            for hh in heads:
                consume(hh, *cur, u % 2)
            cur = nxt
        return cur

    lax.fori_loop(0, n_trips, body, (jnp.int32(1), jnp.int32(0)))

    for hh in heads:
        for i in range(n_tiles):
            acc = acc_sc[hh, i]
            o = acc[:V_HEAD, :] * (1.0 / acc[V_HEAD:V_HEAD + 1, :])
            o_ref[hh, i * tile:(i + 1) * tile, :] = o.T.astype(BF16)


def _flash(q, k, vt, weights, *, batch, seq):
    n_tiles = seq // ATTN_TILE
    head_groups = N_HEADS // FLASH_HEADS
    steps = batch * head_groups
    w2d = [w for w, _, _ in weights]
    slabs = [(n_rows // steps, w.shape[1]) for w, _, n_rows in weights]
    assert all(n_rows % steps == 0 and first % (n_rows // steps) == 0
               for _, first, n_rows in weights)
    slab_maps = [functools.partial(lambda b, g, off: (off + b * head_groups + g, 0),
                                   off=first // (n_rows // steps)) for _, first, n_rows in weights]
    step_row = lambda b, g: (b * head_groups + g, 0)
    heads_of = lambda b, g: (b, g, 0, 0)
    chunked_heads_of = lambda b, g: (b, 0, g, 0, 0)
    o, *w_bf = pl.pallas_call(
        functools.partial(_flash_kernel, n_tiles=n_tiles, n_cast=len(w2d)),
        grid=(batch, head_groups),
        in_specs=[
            pl.BlockSpec((None, n_tiles, FLASH_HEADS, QK_PAD, ATTN_TILE), chunked_heads_of),
            pl.BlockSpec((None, FLASH_HEADS, seq, QK_PAD), heads_of),
            pl.BlockSpec((None, n_tiles, FLASH_HEADS, V_ROWS, ATTN_TILE), chunked_heads_of),
        ] + [pl.BlockSpec(slab, slab_map) for slab, slab_map in zip(slabs, slab_maps)],
        out_specs=[
            pl.BlockSpec((None, FLASH_HEADS, seq, V_HEAD), heads_of),
        ] + [pl.BlockSpec(slab, step_row) for slab in slabs],
        out_shape=[
            jax.ShapeDtypeStruct((batch, N_HEADS, seq, V_HEAD), BF16),
        ] + [jax.ShapeDtypeStruct((n_rows, w.shape[1]), BF16) for w, _, n_rows in weights],
        scratch_shapes=[
            pltpu.VMEM((FLASH_HEADS, 2, ATTN_TILE, ATTN_TILE), F32),
            pltpu.VMEM((FLASH_HEADS, 2, 1, ATTN_TILE), F32),
            pltpu.VMEM((FLASH_HEADS, n_tiles, 1, ATTN_TILE), F32),
            pltpu.VMEM((FLASH_HEADS, n_tiles, V_ROWS, ATTN_TILE), F32),
        ],
        compiler_params=_params("parallel", "parallel"),
        name="flash_attn",
    )(q, k, vt, *w2d)
    return o, w_bf


def _out_proj_kernel(x_ref, o_ref, w_ref, h_ref):
    o = jnp.concatenate([o_ref[h] for h in range(N_HEADS)], axis=-1)
    h_ref[...] = x_ref[...] + jnp.dot(o, w_ref[...], preferred_element_type=F32)


def _out_proj(x, o, w, *, seq):
    t = x.shape[0]
    tiles_per_seq = seq // ROW_TILE
    row = lambda i: (i, 0)
    return pl.pallas_call(
        _out_proj_kernel,
        grid=(t // ROW_TILE,),
        in_specs=[
            pl.BlockSpec((ROW_TILE, D_MODEL), row),
            pl.BlockSpec((None, N_HEADS, ROW_TILE, V_HEAD),
                         lambda i: (i // tiles_per_seq, 0, i % tiles_per_seq, 0)),
            pl.BlockSpec(w.shape, lambda i: (0, 0)),
        ],
        out_specs=pl.BlockSpec((ROW_TILE, D_MODEL), row),
        out_shape=jax.ShapeDtypeStruct((t, D_MODEL), F32),
        compiler_params=_params("parallel"),
        name="out_proj",
    )(x, o, w)


def _mlp_kernel(x_ref, g_ref, wu_ref, wd_ref, fg_ref, o_ref, xn_sc, *, final_norm):
    j = pl.program_id(1)

    @pl.when(j == 0)
    def _():
        x = x_ref[...]
        xn_sc[...] = _rms(x, g_ref[...]).astype(BF16)
        o_ref[...] = x

    for c in range(MLP_FF_TILE // MLP_FF_SUB):
        cs = slice(c * MLP_FF_SUB, (c + 1) * MLP_FF_SUB)
        u = jnp.dot(xn_sc[...], wu_ref[:, cs], preferred_element_type=F32)
        u = jnp.maximum(u, 0.0)
        o_ref[...] += jnp.dot((u * u).astype(BF16), wd_ref[cs, :], preferred_element_type=F32)

    if final_norm:
        @pl.when(j == pl.num_programs(1) - 1)
        def _():
            o_ref[...] = _rms(o_ref[...], fg_ref[...])


def _mlp(x, g, wu, wd, fg, *, wu_layer, wd_layer, final_norm):
    t = x.shape[0]
    row = lambda i, j: (i, 0)
    fixed = lambda i, j: (0, 0)
    return pl.pallas_call(
        functools.partial(_mlp_kernel, final_norm=final_norm),
        grid=(t // MLP_ROW_TILE, D_FF // MLP_FF_TILE),
        in_specs=[
            pl.BlockSpec((MLP_ROW_TILE, D_MODEL), row),
            pl.BlockSpec((1, D_MODEL), fixed),
            pl.BlockSpec((None, D_MODEL, MLP_FF_TILE), lambda i, j: (wu_layer, 0, j)),
            pl.BlockSpec((None, MLP_FF_TILE, D_MODEL), lambda i, j: (wd_layer, j, 0)),
            pl.BlockSpec((1, D_MODEL), fixed),
        ],
        out_specs=pl.BlockSpec((MLP_ROW_TILE, D_MODEL), row),
        out_shape=jax.ShapeDtypeStruct((t, D_MODEL), F32),
        scratch_shapes=[pltpu.VMEM((MLP_ROW_TILE, D_MODEL), BF16)],
        compiler_params=_params("parallel", "arbitrary"),
        name="mlp_final" if final_norm else "mlp",
    )(x, g, wu, wd, fg)


def _pool_mix_kernel(x_ref, halo_ref, g_ref, w_ref, ps_ref, o_ref, buf_sc, *, tiles_per_seq):
    tile = POOL_ROW_TILE
    i = pl.program_id(0)
    x = x_ref[...]
    g = g_ref[...]
    xn = _rms(x, g)
    t_in_seq = i % tiles_per_seq
    halo = jnp.where(t_in_seq == 0, 0.0, _rms(halo_ref[...], g))
    buf_sc[:POOL_HALO, :] = halo
    buf_sc[POOL_HALO:, :] = xn
    t = t_in_seq * tile + lax.broadcasted_iota(jnp.int32, (tile, 1), 0)
    for gi, win in enumerate(POOL_WINDOWS):
        cols = slice(gi * POOL_GROUP, (gi + 1) * POOL_GROUP)
        xg = xn[:, cols]
        acc = buf_sc[:, cols]
        span = 1
        while span < win:
            acc = acc + pltpu.roll(acc, span, 0)
            span *= 2
        acc = acc[POOL_HALO:, :]
        cnt = jnp.minimum(t + 1, win).astype(F32)
        y = (acc / cnt - xg).astype(BF16)
        mix = jnp.dot(y, w_ref[gi], preferred_element_type=F32)
        o_ref[:, cols] = x[:, cols] + mix * ps_ref[:, cols]


def _pool_mix(x, g, w, ps, *, seq):
    t = x.shape[0]
    tiles_per_seq = seq // POOL_ROW_TILE
    halo_blocks = POOL_ROW_TILE // POOL_HALO
    row = lambda i: (i, 0)
    fixed = lambda i: (0, 0)
    return pl.pallas_call(
        functools.partial(_pool_mix_kernel, tiles_per_seq=tiles_per_seq),
        grid=(t // POOL_ROW_TILE,),
        in_specs=[
            pl.BlockSpec((POOL_ROW_TILE, D_MODEL), row),
            pl.BlockSpec((POOL_HALO, D_MODEL), lambda i: (jnp.maximum(i * halo_blocks - 1, 0), 0)),
            pl.BlockSpec((1, D_MODEL), fixed),
            pl.BlockSpec(w.shape, lambda i: (0, 0, 0)),
            pl.BlockSpec((1, D_MODEL), fixed),
        ],
        out_specs=pl.BlockSpec((POOL_ROW_TILE, D_MODEL), row),
        out_shape=jax.ShapeDtypeStruct((t, D_MODEL), F32),
        scratch_shapes=[pltpu.VMEM((POOL_ROW_TILE + POOL_HALO, D_MODEL), F32)],
        compiler_params=_params("parallel"),
        name="pool_mix",
    )(x, x, g, w, ps)


def kernel(x, positions, ln_mix, ln_mlp, w_dqkv, q_norm, kv_norm, w_uq, w_ukv, w_o,
           w_pool, pool_scale, w_up, w_down, final_norm):
    batch, seq, d = x.shape
    assert d == D_MODEL and all(
        seq % tile == 0 for tile in (ATTN_TILE, ROW_TILE, DOWN_ROW_TILE, POOL_ROW_TILE))
    t = batch * seq
    h = x.reshape(t, d)

    w_dqkv_p = jnp.pad(w_dqkv[0], ((0, 0), (0, LANES - QK_ROPE))).astype(BF16)
    wqt = w_uq[0].T.astype(BF16)
    wkv = w_ukv[0].reshape(KV_LORA, N_HEADS, QK_NOPE + V_HEAD)
    wk = wkv[:, :, :QK_NOPE].reshape(KV_LORA, N_HEADS * QK_NOPE).astype(BF16)
    wvt = wkv[:, :, QK_NOPE:].reshape(KV_LORA, N_HEADS * V_HEAD).T.astype(BF16)
    wp = w_pool[0].astype(BF16)

    inv_freq = 1.0 / (ROPE_THETA ** (jnp.arange(0, QK_ROPE, 2, dtype=F32) / QK_ROPE))
    pos_rows = positions.reshape(t // ATTN_TILE, 1, ATTN_TILE)
    invf_col = inv_freq.reshape(QK_ROPE // 2, 1)
    scale = math.log2(math.e) / math.sqrt(QK_HEAD)

    row = lambda v: v.reshape(1, -1)

    n_layers = w_up.shape[0]
    wu_2d = w_up.reshape(n_layers * D_MODEL, D_FF)
    wd_2d = w_down.reshape(n_layers * D_FF, D_MODEL)
    cq, ckv, kr, wd1 = _attn_down(h, row(ln_mix[0]), w_dqkv_p, row(q_norm[0]), row(kv_norm[0]),
                                  positions, invf_col, wd_2d, D_FF, D_FF)
    qt, k, vt = _attn_up(cq, ckv, kr, pos_rows, invf_col, wqt, wk, wvt,
                         batch=batch, seq=seq, scale=scale)
    o, (wu, wd0, wo) = _flash(
        qt, k, vt, ((wu_2d, 0, wu_2d.shape[0]), (wd_2d, 0, D_FF), (w_o[0], 0, w_o.shape[1])),
        batch=batch, seq=seq)
    wu = wu.reshape(w_up.shape)
    h = _out_proj(h, o, wo, seq=seq)
    h = _mlp(h, row(ln_mlp[0]), wu, wd0[None], row(final_norm),
             wu_layer=0, wd_layer=0, final_norm=False)

    h = _pool_mix(h, row(ln_mix[1]), wp, row(pool_scale[0]), seq=seq)
    h = _mlp(h, row(ln_mlp[1]), wu, wd1[None], row(final_norm),
             wu_layer=1, wd_layer=0, final_norm=True)
    return h.reshape(batch, seq, d)
```

```python
import functools
import math

import jax
import jax.numpy as jnp
from jax import lax
from jax.experimental import pallas as pl
from jax.experimental.pallas import tpu as pltpu

D_MODEL = 2048
N_HEADS = 16
Q_LORA = 512
KV_LORA = 512
QK_NOPE = 128
QK_ROPE = 64
V_HEAD = 128
QK_HEAD = QK_NOPE + QK_ROPE
ROPE_THETA = 10000.0
POOL_WINDOWS = (2, 4, 8, 16)
POOL_GROUP = D_MODEL // len(POOL_WINDOWS)
D_FF = 4 * D_MODEL
EPS = 1e-6

LANES = 128
BF16_ROWS = 16
QK_PAD = 2 * LANES
V_ROWS = V_HEAD + 16
POOL_HALO = 16
VMEM_LIMIT_BYTES = 56 * 1024 * 1024

F32 = jnp.float32
BF16 = jnp.bfloat16

ROW_TILE = 512
DOWN_ROW_TILE = 1024
DOWN_SUB_ROWS = 256
POOL_ROW_TILE = 1024
ATTN_TILE = 512
FLASH_HEADS = 2
FLASH_UNROLL = 14
MLP_ROW_TILE = 512
MLP_FF_TILE = 2048
MLP_FF_SUB = 1024


def _params(*semantics):
    return pltpu.CompilerParams(dimension_semantics=semantics,
                                vmem_limit_bytes=VMEM_LIMIT_BYTES)


def _rms(x, g):
    ms = jnp.mean(x * x, axis=-1, keepdims=True)
    return x * lax.rsqrt(ms + EPS) * g


def _rope_tables(pos_row, invf_col):
    ang = invf_col * pos_row.astype(F32)
    cos = jnp.cos(ang)
    sin = jnp.sin(ang)
    z = jnp.zeros_like(cos)
    pad = jnp.zeros((LANES - QK_ROPE, ang.shape[1]), F32)
    cos_t = jnp.concatenate([cos, cos, pad], axis=0)
    sin_lo_t = jnp.concatenate([-sin, z, pad], axis=0)
    sin_hi_t = jnp.concatenate([z, sin, pad], axis=0)
    return cos_t.T, sin_lo_t.T, sin_hi_t.T


def _rope(x, cos, sin_lo, sin_hi):
    half = QK_ROPE // 2
    return (x * cos + pltpu.roll(x, LANES - half, 1) * sin_lo
            + pltpu.roll(x, half, 1) * sin_hi)


def _attn_down_kernel(x_ref, g_ref, w_ref, qn_ref, kvn_ref, pos_ref, invf_ref, wc_ref,
                      cq_ref, ckv_ref, kr_ref, wc_bf_ref):
    wc_bf_ref[...] = wc_ref[...].astype(BF16)
    cos, sin_lo, sin_hi = _rope_tables(pos_ref[...], invf_ref[...])
    for r in range(DOWN_ROW_TILE // DOWN_SUB_ROWS):
        rs = slice(r * DOWN_SUB_ROWS, (r + 1) * DOWN_SUB_ROWS)
        xn = _rms(x_ref[rs, :], g_ref[...]).astype(BF16)
        c = jnp.dot(xn, w_ref[...], preferred_element_type=F32)
        cq_ref[rs, :] = _rms(c[:, :Q_LORA], qn_ref[...]).astype(BF16)
        ckv_ref[rs, :] = _rms(c[:, Q_LORA:Q_LORA + KV_LORA], kvn_ref[...]).astype(BF16)
        kr_ref[rs, :] = _rope(c[:, Q_LORA + KV_LORA:], cos[rs, :], sin_lo[rs, :],
                              sin_hi[rs, :]).astype(BF16)


def _attn_down(x, g, w, qn, kvn, positions, invf_col, w_cast, cast_first_row, cast_rows):
    t = x.shape[0]
    n = w.shape[1]
    steps = t // DOWN_ROW_TILE
    row = lambda i: (i, 0)
    fixed = lambda i: (0, 0)
    pos_rows = positions.reshape(steps, 1, DOWN_ROW_TILE)
    slab = (cast_rows // steps, w_cast.shape[1])
    assert cast_rows % (steps * BF16_ROWS) == 0 and cast_first_row % slab[0] == 0
    slab_offset = cast_first_row // slab[0]
    return pl.pallas_call(
        _attn_down_kernel,
        grid=(steps,),
        in_specs=[
            pl.BlockSpec((DOWN_ROW_TILE, D_MODEL), row),
            pl.BlockSpec((1, D_MODEL), fixed),
            pl.BlockSpec((D_MODEL, n), fixed),
            pl.BlockSpec((1, Q_LORA), fixed),
            pl.BlockSpec((1, KV_LORA), fixed),
            pl.BlockSpec((None, 1, DOWN_ROW_TILE), lambda i: (i, 0, 0)),
            pl.BlockSpec(invf_col.shape, fixed),
            pl.BlockSpec(slab, lambda i: (slab_offset + i, 0)),
        ],
        out_specs=[
            pl.BlockSpec((DOWN_ROW_TILE, Q_LORA), row),
            pl.BlockSpec((DOWN_ROW_TILE, KV_LORA), row),
            pl.BlockSpec((DOWN_ROW_TILE, LANES), row),
            pl.BlockSpec(slab, row),
        ],
        out_shape=[
            jax.ShapeDtypeStruct((t, Q_LORA), BF16),
            jax.ShapeDtypeStruct((t, KV_LORA), BF16),
            jax.ShapeDtypeStruct((t, LANES), BF16),
            jax.ShapeDtypeStruct((cast_rows, w_cast.shape[1]), BF16),
        ],
        compiler_params=_params("parallel"),
        name="attn_down",
    )(x, g, w, qn, kvn, pos_rows, invf_col, w_cast)


def _attn_up_kernel(cq_ref, ckv_ref, pos_ref, invf_ref, wqt_ref, wk_ref, wvt_ref,
                    qt_ref, k_ref, vt_ref, *, scale):
    cq = cq_ref[...]
    ckv = ckv_ref[...]
    half = QK_ROPE // 2
    ang = invf_ref[...] * pos_ref[...].astype(F32)
    cos = jnp.cos(ang)
    sin = jnp.sin(ang)
    group = 4
    for c in range(N_HEADS // group):
        qt = lax.dot_general(wqt_ref[c * group * QK_HEAD:(c + 1) * group * QK_HEAD, :], cq,
                             (((1,), (1,)), ((), ())), preferred_element_type=F32)
        for hh in range(group):
            h = c * group + hh
            qh = qt[hh * QK_HEAD:(hh + 1) * QK_HEAD, :]
            x1 = qh[QK_NOPE:QK_NOPE + half, :]
            x2 = qh[QK_NOPE + half:, :]
            qt_ref[h, :QK_NOPE, :] = (qh[:QK_NOPE, :] * scale).astype(BF16)
            qt_ref[h, QK_NOPE:QK_NOPE + half, :] = ((x1 * cos - x2 * sin) * scale).astype(BF16)
            qt_ref[h, QK_NOPE + half:, :] = ((x2 * cos + x1 * sin) * scale).astype(BF16)
    for p in range(N_HEADS // 2):
        kk = jnp.dot(ckv, wk_ref[:, p * QK_PAD:(p + 1) * QK_PAD], preferred_element_type=F32)
        k_ref[2 * p] = kk[:, :QK_NOPE].astype(BF16)
        k_ref[2 * p + 1] = kk[:, QK_NOPE:].astype(BF16)
    pad_rows = V_ROWS - V_HEAD
    ones_row = (lax.broadcasted_iota(jnp.int32, (pad_rows, ATTN_TILE), 0) == 0).astype(BF16)
    group = 4
    for c in range(N_HEADS // group):
        vt = lax.dot_general(wvt_ref[c * group * V_HEAD:(c + 1) * group * V_HEAD, :], ckv,
                             (((1,), (1,)), ((), ())), preferred_element_type=F32)
        for hh in range(group):
            vt_ref[c * group + hh, :V_HEAD, :] = vt[hh * V_HEAD:(hh + 1) * V_HEAD, :].astype(BF16)
            vt_ref[c * group + hh, V_HEAD:, :] = ones_row


def _attn_up(cq, ckv, pos_rows, invf_col, wqt, wk, wvt, *, batch, seq, scale):
    t = cq.shape[0]
    tiles_per_seq = seq // ATTN_TILE
    row = lambda i: (i, 0)
    fixed = lambda i: (0, 0)
    head_major = lambda i: (i // tiles_per_seq, 0, i % tiles_per_seq, 0)
    chunk_major = lambda i: (i // tiles_per_seq, i % tiles_per_seq, 0, 0, 0)
    return pl.pallas_call(
        functools.partial(_attn_up_kernel, scale=scale),
        grid=(t // ATTN_TILE,),
        in_specs=[
            pl.BlockSpec((ATTN_TILE, Q_LORA), row),
            pl.BlockSpec((ATTN_TILE, KV_LORA), row),
            pl.BlockSpec((None, 1, ATTN_TILE), lambda i: (i, 0, 0)),
            pl.BlockSpec(invf_col.shape, fixed),
            pl.BlockSpec(wqt.shape, fixed),
            pl.BlockSpec(wk.shape, fixed),
            pl.BlockSpec(wvt.shape, fixed),
        ],
        out_specs=[
            pl.BlockSpec((None, None, N_HEADS, QK_HEAD, ATTN_TILE), chunk_major),
            pl.BlockSpec((None, N_HEADS, ATTN_TILE, QK_NOPE), head_major),
            pl.BlockSpec((None, None, N_HEADS, V_ROWS, ATTN_TILE), chunk_major),
        ],
        out_shape=[
            jax.ShapeDtypeStruct((batch, tiles_per_seq, N_HEADS, QK_HEAD, ATTN_TILE), BF16),
            jax.ShapeDtypeStruct((batch, N_HEADS, seq, QK_NOPE), BF16),
            jax.ShapeDtypeStruct((batch, tiles_per_seq, N_HEADS, V_ROWS, ATTN_TILE), BF16),
        ],
        compiler_params=_params("parallel"),
        name="attn_up",
    )(cq, ckv, pos_rows, invf_col, wqt, wk, wvt)


def _flash_kernel(qt_ref, kn_ref, kr_ref, vt_ref, *rest, n_tiles, n_cast):
    w_refs = rest[:n_cast]
    o_ref = rest[n_cast]
    w_bf_refs = rest[n_cast + 1:2 * n_cast + 1]
    s_sc, cmax_sc, m_sc, acc_sc = rest[2 * n_cast + 1:]
    tile = ATTN_TILE
    half = tile // 2

    def rows(idx):
        if isinstance(idx, int):
            return pl.ds(idx * tile, tile)
        return pl.ds(pl.multiple_of(idx * tile, tile), tile)

    def keys(hh, row_slice):
        return jnp.concatenate([kn_ref[hh, row_slice, :], kr_ref[row_slice, :]], axis=-1)

    def queries_t(i, hh, lanes=slice(None)):
        qt = qt_ref[i, hh, :, lanes]
        return jnp.concatenate([qt, jnp.zeros((QK_PAD - QK_HEAD, qt.shape[1]), BF16)], axis=0)

    def produce(hh, i, j, slot):
        s = jnp.dot(keys(hh, rows(j)), queries_t(i, hh), preferred_element_type=F32)
        s_sc[hh, slot] = s
        cmax_sc[hh, slot] = jnp.max(s, axis=0, keepdims=True)

    def consume(hh, i, j, slot):
        m_prev = m_sc[hh, i]
        m_new = jnp.maximum(m_prev, cmax_sc[hh, slot])
        alpha = jnp.exp2(m_prev - m_new)
        p = jnp.exp2(s_sc[hh, slot] - m_new).astype(BF16)
        m_sc[hh, i] = m_new
        acc_sc[hh, i] = alpha * acc_sc[hh, i] + jnp.dot(vt_ref[j, hh], p,
                                                        preferred_element_type=F32)

    def produce_diagonal(hh, i, slot):
        s_lo = jnp.dot(keys(hh, pl.ds(i * tile, half)), queries_t(i, hh),
                       preferred_element_type=F32)
        s_hi = jnp.dot(keys(hh, pl.ds(i * tile + half, half)),
                       queries_t(i, hh, slice(half, None)),
                       preferred_element_type=F32)
        kpos = lax.broadcasted_iota(jnp.int32, (half, half), 0)
        qpos = lax.broadcasted_iota(jnp.int32, (half, half), 1)
        causal = kpos <= qpos
        s_ll = jnp.where(causal, s_lo[:, :half], -jnp.inf)
        s_hh = jnp.where(causal, s_hi, -jnp.inf)
        s_sc[hh, slot, :half, :half] = s_ll
        s_sc[hh, slot, :half, half:] = s_lo[:, half:]
        s_sc[hh, slot, half:, half:] = s_hh
        cmax_sc[hh, slot, :, :half] = jnp.max(s_ll, axis=0, keepdims=True)
        cmax_sc[hh, slot, :, half:] = jnp.maximum(
            jnp.max(s_lo[:, half:], axis=0, keepdims=True), jnp.max(s_hh, axis=0, keepdims=True))

    def consume_diagonal(hh, i, slot):
        m = cmax_sc[hh, slot]
        p_lo = jnp.exp2(s_sc[hh, slot, :half, :] - m).astype(BF16)
        p_hi = jnp.exp2(s_sc[hh, slot, half:, half:] - m[:, half:]).astype(BF16)
        pv = jnp.dot(vt_ref[i, hh, :, :half], p_lo, preferred_element_type=F32)
        pv_hi = jnp.dot(vt_ref[i, hh, :, half:], p_hi, preferred_element_type=F32)
        m_sc[hh, i] = m
        acc_sc[hh, i, :, :half] = pv[:, :half]
        acc_sc[hh, i, :, half:] = pv[:, half:] + pv_hi

    assert n_tiles % 2 == 0
    heads = range(FLASH_HEADS)
    for hh in heads:
        produce_diagonal(hh, 0, 0)
    for i in range(n_tiles):
        for hh in heads:
            if i + 1 < n_tiles:
                produce_diagonal(hh, i + 1, (i + 1) % 2)
            else:
                produce(hh, 1, 0, (i + 1) % 2)
        for hh in heads:
            consume_diagonal(hh, i, i % 2)

    def succ(i, j):
        wrap = j + 1 >= i
        return jnp.where(wrap, i + 1, i), jnp.where(wrap, 0, j + 1)

    n_off = n_tiles * (n_tiles - 1) // 2
    n_trips = n_off // FLASH_UNROLL
    assert n_off % FLASH_UNROLL == 0 and FLASH_UNROLL % 2 == 0

    def body(trip, cur):
        for w_ref, w_bf_ref in zip(w_refs, w_bf_refs):
            n_rows = w_ref.shape[0] // n_trips
            assert n_rows % BF16_ROWS == 0
            rs = pl.ds(pl.multiple_of(trip * n_rows, n_rows), n_rows)
            w_bf_ref[rs, :] = w_ref[rs, :].astype(BF16)
        for u in range(FLASH_UNROLL):
            nxt = succ(*cur)
            for hh in heads:
                produce(hh, jnp.minimum(nxt[0], n_tiles - 1), jnp.minimum(nxt[1], n_tiles - 2),
                        1 - u % 2)
            for hh in heads:
                consume(hh, *cur, u % 2)
            cur = nxt
        return cur

    lax.fori_loop(0, n_trips, body, (jnp.int32(1), jnp.int32(0)))

    for hh in heads:
        for i in range(n_tiles):
            acc = acc_sc[hh, i]
            o = acc[:V_HEAD, :] * (1.0 / acc[V_HEAD:V_HEAD + 1, :])
            o_ref[hh, i * tile:(i + 1) * tile, :] = o.T.astype(BF16)


def _flash(qt, kn, kr, vt, weights, *, batch, seq):
    n_tiles = seq // ATTN_TILE
    head_groups = N_HEADS // FLASH_HEADS
    steps = batch * head_groups
    w2d = [w for w, _, _ in weights]
    slabs = [(n_rows // steps, w.shape[1]) for w, _, n_rows in weights]
    assert all(n_rows % steps == 0 and first % (n_rows // steps) == 0
               for _, first, n_rows in weights)
    slab_maps = [functools.partial(lambda b, g, off: (off + b * head_groups + g, 0),
                                   off=first // (n_rows // steps)) for _, first, n_rows in weights]
    step_row = lambda b, g: (b * head_groups + g, 0)
    heads_of = lambda b, g: (b, g, 0, 0)
    chunked_heads_of = lambda b, g: (b, 0, g, 0, 0)
    o, *w_bf = pl.pallas_call(
        functools.partial(_flash_kernel, n_tiles=n_tiles, n_cast=len(w2d)),
        grid=(batch, head_groups),
        in_specs=[
            pl.BlockSpec((None, n_tiles, FLASH_HEADS, QK_HEAD, ATTN_TILE), chunked_heads_of),
            pl.BlockSpec((None, FLASH_HEADS, seq, QK_NOPE), heads_of),
            pl.BlockSpec((seq, LANES), lambda b, g: (b, 0)),
            pl.BlockSpec((None, n_tiles, FLASH_HEADS, V_ROWS, ATTN_TILE), chunked_heads_of),
        ] + [pl.BlockSpec(slab, slab_map) for slab, slab_map in zip(slabs, slab_maps)],
        out_specs=[
            pl.BlockSpec((None, FLASH_HEADS, seq, V_HEAD), heads_of),
        ] + [pl.BlockSpec(slab, step_row) for slab in slabs],
        out_shape=[
            jax.ShapeDtypeStruct((batch, N_HEADS, seq, V_HEAD), BF16),
        ] + [jax.ShapeDtypeStruct((n_rows, w.shape[1]), BF16) for w, _, n_rows in weights],
        scratch_shapes=[
            pltpu.VMEM((FLASH_HEADS, 2, ATTN_TILE, ATTN_TILE), F32),
            pltpu.VMEM((FLASH_HEADS, 2, 1, ATTN_TILE), F32),
            pltpu.VMEM((FLASH_HEADS, n_tiles, 1, ATTN_TILE), F32),
            pltpu.VMEM((FLASH_HEADS, n_tiles, V_ROWS, ATTN_TILE), F32),
        ],
        compiler_params=_params("parallel", "parallel"),
        name="flash_attn",
    )(qt, kn, kr, vt, *w2d)
    return o, w_bf


def _out_proj_kernel(x_ref, o_ref, w_ref, h_ref):
    o = jnp.concatenate([o_ref[h] for h in range(N_HEADS)], axis=-1)
    h_ref[...] = x_ref[...] + jnp.dot(o, w_ref[...], preferred_element_type=F32)


def _out_proj(x, o, w, *, seq):
    t = x.shape[0]
    tiles_per_seq = seq // ROW_TILE
    row = lambda i: (i, 0)
    return pl.pallas_call(
        _out_proj_kernel,
        grid=(t // ROW_TILE,),
        in_specs=[
            pl.BlockSpec((ROW_TILE, D_MODEL), row),
            pl.BlockSpec((None, N_HEADS, ROW_TILE, V_HEAD),
                         lambda i: (i // tiles_per_seq, 0, i % tiles_per_seq, 0)),
            pl.BlockSpec(w.shape, lambda i: (0, 0)),
        ],
        out_specs=pl.BlockSpec((ROW_TILE, D_MODEL), row),
        out_shape=jax.ShapeDtypeStruct((t, D_MODEL), F32),
        compiler_params=_params("parallel"),
        name="out_proj",
    )(x, o, w)


def _mlp_kernel(x_ref, g_ref, wu_ref, wd_ref, fg_ref, o_ref, xn_sc, *, final_norm):
    j = pl.program_id(1)

    @pl.when(j == 0)
    def _():
        x = x_ref[...]
        xn_sc[...] = _rms(x, g_ref[...]).astype(BF16)
        o_ref[...] = x

    for c in range(MLP_FF_TILE // MLP_FF_SUB):
        cs = slice(c * MLP_FF_SUB, (c + 1) * MLP_FF_SUB)
        u = jnp.dot(xn_sc[...], wu_ref[:, cs], preferred_element_type=F32)
        u = jnp.maximum(u, 0.0)
        o_ref[...] += jnp.dot((u * u).astype(BF16), wd_ref[cs, :], preferred_element_type=F32)

    if final_norm:
        @pl.when(j == pl.num_programs(1) - 1)
        def _():
            o_ref[...] = _rms(o_ref[...], fg_ref[...])


def _mlp(x, g, wu, wd, fg, *, wu_layer, wd_layer, final_norm):
    t = x.shape[0]
    row = lambda i, j: (i, 0)
    fixed = lambda i, j: (0, 0)
    return pl.pallas_call(
        functools.partial(_mlp_kernel, final_norm=final_norm),
        grid=(t // MLP_ROW_TILE, D_FF // MLP_FF_TILE),
        in_specs=[
            pl.BlockSpec((MLP_ROW_TILE, D_MODEL), row),
            pl.BlockSpec((1, D_MODEL), fixed),
            pl.BlockSpec((None, D_MODEL, MLP_FF_TILE), lambda i, j: (wu_layer, 0, j)),
            pl.BlockSpec((None, MLP_FF_TILE, D_MODEL), lambda i, j: (wd_layer, j, 0)),
            pl.BlockSpec((1, D_MODEL), fixed),
        ],
        out_specs=pl.BlockSpec((MLP_ROW_TILE, D_MODEL), row),
        out_shape=jax.ShapeDtypeStruct((t, D_MODEL), F32),
        scratch_shapes=[pltpu.VMEM((MLP_ROW_TILE, D_MODEL), BF16)],
        compiler_params=_params("parallel", "arbitrary"),
        name="mlp_final" if final_norm else "mlp",
    )(x, g, wu, wd, fg)


def _pool_mix_kernel(x_ref, halo_ref, g_ref, w_ref, ps_ref, o_ref, buf_sc, *, tiles_per_seq):
    tile = POOL_ROW_TILE
    i = pl.program_id(0)
    x = x_ref[...]
    g = g_ref[...]
    xn = _rms(x, g)
    t_in_seq = i % tiles_per_seq
    halo = jnp.where(t_in_seq == 0, 0.0, _rms(halo_ref[...], g))
    buf_sc[:POOL_HALO, :] = halo
    buf_sc[POOL_HALO:, :] = xn
    t = t_in_seq * tile + lax.broadcasted_iota(jnp.int32, (tile, 1), 0)
    for gi, win in enumerate(POOL_WINDOWS):
        cols = slice(gi * POOL_GROUP, (gi + 1) * POOL_GROUP)
        xg = xn[:, cols]
        acc = buf_sc[:, cols]
        span = 1
        while span < win:
            acc = acc + pltpu.roll(acc, span, 0)
            span *= 2
        acc = acc[POOL_HALO:, :]
        cnt = jnp.minimum(t + 1, win).astype(F32)
        y = (acc / cnt - xg).astype(BF16)
        mix = jnp.dot(y, w_ref[gi], preferred_element_type=F32)
        o_ref[:, cols] = x[:, cols] + mix * ps_ref[:, cols]


def _pool_mix(x, g, w, ps, *, seq):
    t = x.shape[0]
    tiles_per_seq = seq // POOL_ROW_TILE
    halo_blocks = POOL_ROW_TILE // POOL_HALO
    row = lambda i: (i, 0)
    fixed = lambda i: (0, 0)
    return pl.pallas_call(
        functools.partial(_pool_mix_kernel, tiles_per_seq=tiles_per_seq),
        grid=(t // POOL_ROW_TILE,),
        in_specs=[
            pl.BlockSpec((POOL_ROW_TILE, D_MODEL), row),
            pl.BlockSpec((POOL_HALO, D_MODEL), lambda i: (jnp.maximum(i * halo_blocks - 1, 0), 0)),
            pl.BlockSpec((1, D_MODEL), fixed),
            pl.BlockSpec(w.shape, lambda i: (0, 0, 0)),
            pl.BlockSpec((1, D_MODEL), fixed),
        ],
        out_specs=pl.BlockSpec((POOL_ROW_TILE, D_MODEL), row),
        out_shape=jax.ShapeDtypeStruct((t, D_MODEL), F32),
        scratch_shapes=[pltpu.VMEM((POOL_ROW_TILE + POOL_HALO, D_MODEL), F32)],
        compiler_params=_params("parallel"),
        name="pool_mix",
    )(x, x, g, w, ps)


def kernel(x, positions, ln_mix, ln_mlp, w_dqkv, q_norm, kv_norm, w_uq, w_ukv, w_o,
           w_pool, pool_scale, w_up, w_down, final_norm):
    batch, seq, d = x.shape
    assert d == D_MODEL and all(
        seq % tile == 0 for tile in (ATTN_TILE, ROW_TILE, DOWN_ROW_TILE, POOL_ROW_TILE))
    t = batch * seq
    h = x.reshape(t, d)

    w_dqkv_p = jnp.pad(w_dqkv[0], ((0, 0), (0, LANES - QK_ROPE))).astype(BF16)
    wqt = w_uq[0].T.astype(BF16)
    wkv = w_ukv[0].reshape(KV_LORA, N_HEADS, QK_NOPE + V_HEAD)
    wk = wkv[:, :, :QK_NOPE].reshape(KV_LORA, N_HEADS * QK_NOPE).astype(BF16)
    wvt = wkv[:, :, QK_NOPE:].reshape(KV_LORA, N_HEADS * V_HEAD).T.astype(BF16)
    wp = w_pool[0].astype(BF16)

    inv_freq = 1.0 / (ROPE_THETA ** (jnp.arange(0, QK_ROPE, 2, dtype=F32) / QK_ROPE))
    pos_rows = positions.reshape(t // ATTN_TILE, 1, ATTN_TILE)
    invf_col = inv_freq.reshape(QK_ROPE // 2, 1)
    scale = math.log2(math.e) / math.sqrt(QK_HEAD)

    row = lambda v: v.reshape(1, -1)

    n_layers = w_up.shape[0]
    wu_2d = w_up.reshape(n_layers * D_MODEL, D_FF)
    wd_2d = w_down.reshape(n_layers * D_FF, D_MODEL)
    cq, ckv, kr, wd1 = _attn_down(h, row(ln_mix[0]), w_dqkv_p, row(q_norm[0]), row(kv_norm[0]),
                                  positions, invf_col, wd_2d, D_FF, D_FF)
    qt, kn, vt = _attn_up(cq, ckv, pos_rows, invf_col, wqt, wk, wvt,
                          batch=batch, seq=seq, scale=scale)
    o, (wu, wd0, wo) = _flash(
        qt, kn, kr, vt,
        ((wu_2d, 0, wu_2d.shape[0]), (wd_2d, 0, D_FF), (w_o[0], 0, w_o.shape[1])),
        batch=batch, seq=seq)
    wu = wu.reshape(w_up.shape)
    h = _out_proj(h, o, wo, seq=seq)
    h = _mlp(h, row(ln_mlp[0]), wu, wd0[None], row(final_norm),
             wu_layer=0, wd_layer=0, final_norm=False)

    h = _pool_mix(h, row(ln_mix[1]), wp, row(pool_scale[0]), seq=seq)
    h = _mlp(h, row(ln_mlp[1]), wu, wd1[None], row(final_norm),
             wu_layer=1, wd_layer=0, final_norm=True)
    return h.reshape(batch, seq, d)
```

```python
import functools
import math

import jax
import jax.numpy as jnp
from jax import lax
from jax.experimental import pallas as pl
from jax.experimental.pallas import tpu as pltpu

D_MODEL = 2048
N_HEADS = 16
Q_LORA = 512
KV_LORA = 512
QK_NOPE = 128
QK_ROPE = 64
V_HEAD = 128
QK_HEAD = QK_NOPE + QK_ROPE
ROPE_THETA = 10000.0
POOL_WINDOWS = (2, 4, 8, 16)
POOL_GROUP = D_MODEL // len(POOL_WINDOWS)
D_FF = 4 * D_MODEL
EPS = 1e-6

LANES = 128
BF16_ROWS = 16
QK_PAD = 2 * LANES
V_ROWS = V_HEAD + BF16_ROWS
POOL_HALO = 16
VMEM_LIMIT_BYTES = 56 * 1024 * 1024

F32 = jnp.float32
BF16 = jnp.bfloat16

ROW_TILE = 512
OUT_ROW_TILE = 1024
DOWN_ROW_TILE = 1024
DOWN_SUB_ROWS = 256
POOL_ROW_TILE = 1024
ATTN_TILE = 512
UP_CHUNKS = 2
FLASH_HEADS = 2
FLASH_UNROLL = 14
MLP_ROW_TILE = 512
MLP_FF_TILE = 2048
MLP_FF_SUB = 1024


def _params(*semantics):
    return pltpu.CompilerParams(dimension_semantics=semantics,
                                vmem_limit_bytes=VMEM_LIMIT_BYTES)


def _rms(x, g):
    ms = jnp.mean(x * x, axis=-1, keepdims=True)
    return x * lax.rsqrt(ms + EPS) * g


def _rope_tables(pos_row, invf_col):
    ang = invf_col * pos_row.astype(F32)
    cos = jnp.cos(ang)
    sin = jnp.sin(ang)
    z = jnp.zeros_like(cos)
    pad = jnp.zeros((LANES - QK_ROPE, ang.shape[1]), F32)
    cos_t = jnp.concatenate([cos, cos, pad], axis=0)
    sin_lo_t = jnp.concatenate([-sin, z, pad], axis=0)
    sin_hi_t = jnp.concatenate([z, sin, pad], axis=0)
    return cos_t.T, sin_lo_t.T, sin_hi_t.T


def _rope(x, cos, sin_lo, sin_hi):
    half = QK_ROPE // 2
    return (x * cos + pltpu.roll(x, LANES - half, 1) * sin_lo
            + pltpu.roll(x, half, 1) * sin_hi)


def _attn_down_kernel(x_ref, g_ref, w_ref, qn_ref, kvn_ref, pos_ref, invf_ref, wc_ref,
                      cq_ref, ckv_ref, kr_ref, wc_bf_ref):
    wc_bf_ref[...] = wc_ref[...].astype(BF16)
    cos, sin_lo, sin_hi = _rope_tables(pos_ref[...], invf_ref[...])
    for r in range(DOWN_ROW_TILE // DOWN_SUB_ROWS):
        rs = slice(r * DOWN_SUB_ROWS, (r + 1) * DOWN_SUB_ROWS)
        xn = _rms(x_ref[rs, :], g_ref[...]).astype(BF16)
        c = jnp.dot(xn, w_ref[...], preferred_element_type=F32)
        cq_ref[rs, :] = _rms(c[:, :Q_LORA], qn_ref[...]).astype(BF16)
        ckv_ref[rs, :] = _rms(c[:, Q_LORA:Q_LORA + KV_LORA], kvn_ref[...]).astype(BF16)
        kr_ref[rs, :] = _rope(c[:, Q_LORA + KV_LORA:], cos[rs, :], sin_lo[rs, :],
                              sin_hi[rs, :]).astype(BF16)


def _attn_down(x, g, w, qn, kvn, positions, invf_col, w_cast, cast_first_row, cast_rows):
    t = x.shape[0]
    n = w.shape[1]
    steps = t // DOWN_ROW_TILE
    row = lambda i: (i, 0)
    fixed = lambda i: (0, 0)
    pos_rows = positions.reshape(steps, 1, DOWN_ROW_TILE)
    slab = (cast_rows // steps, w_cast.shape[1])
    assert cast_rows % (steps * BF16_ROWS) == 0 and cast_first_row % slab[0] == 0
    slab_offset = cast_first_row // slab[0]
    return pl.pallas_call(
        _attn_down_kernel,
        grid=(steps,),
        in_specs=[
            pl.BlockSpec((DOWN_ROW_TILE, D_MODEL), row),
            pl.BlockSpec((1, D_MODEL), fixed),
            pl.BlockSpec((D_MODEL, n), fixed),
            pl.BlockSpec((1, Q_LORA), fixed),
            pl.BlockSpec((1, KV_LORA), fixed),
            pl.BlockSpec((None, 1, DOWN_ROW_TILE), lambda i: (i, 0, 0)),
            pl.BlockSpec(invf_col.shape, fixed),
            pl.BlockSpec(slab, lambda i: (slab_offset + i, 0)),
        ],
        out_specs=[
            pl.BlockSpec((DOWN_ROW_TILE, Q_LORA), row),
            pl.BlockSpec((DOWN_ROW_TILE, KV_LORA), row),
            pl.BlockSpec((DOWN_ROW_TILE, LANES), row),
            pl.BlockSpec(slab, row),
        ],
        out_shape=[
            jax.ShapeDtypeStruct((t, Q_LORA), BF16),
            jax.ShapeDtypeStruct((t, KV_LORA), BF16),
            jax.ShapeDtypeStruct((t, LANES), BF16),
            jax.ShapeDtypeStruct((cast_rows, w_cast.shape[1]), BF16),
        ],
        compiler_params=_params("parallel"),
        name="attn_down",
    )(x, g, w, qn, kvn, pos_rows, invf_col, w_cast)


def _attn_up_kernel(cq_ref, ckv_ref, pos_ref, invf_ref, wqt_ref, wk_ref, wvt_ref,
                    qt_ref, k_ref, vt_ref, *, scale):
    half = QK_ROPE // 2
    pad_rows = V_ROWS - V_HEAD
    ones_row = (lax.broadcasted_iota(jnp.int32, (pad_rows, ATTN_TILE), 0) == 0).astype(BF16)
    group = 4
    for ch in range(UP_CHUNKS):
        rs = slice(ch * ATTN_TILE, (ch + 1) * ATTN_TILE)
        cq = cq_ref[rs, :]
        ckv = ckv_ref[rs, :]
        ang = invf_ref[...] * pos_ref[ch].astype(F32)
        cos = jnp.cos(ang)
        sin = jnp.sin(ang)
        for c in range(N_HEADS // group):
            qt = lax.dot_general(wqt_ref[c * group * QK_HEAD:(c + 1) * group * QK_HEAD, :], cq,
                                 (((1,), (1,)), ((), ())), preferred_element_type=F32)
            for hh in range(group):
                h = c * group + hh
                qh = qt[hh * QK_HEAD:(hh + 1) * QK_HEAD, :]
                x1 = qh[QK_NOPE:QK_NOPE + half, :]
                x2 = qh[QK_NOPE + half:, :]
                qt_ref[ch, h, :QK_NOPE, :] = (qh[:QK_NOPE, :] * scale).astype(BF16)
                qt_ref[ch, h, QK_NOPE:QK_NOPE + half, :] = (
                    (x1 * cos - x2 * sin) * scale).astype(BF16)
                qt_ref[ch, h, QK_NOPE + half:, :] = ((x2 * cos + x1 * sin) * scale).astype(BF16)
        for p in range(N_HEADS // 2):
            kk = jnp.dot(ckv, wk_ref[:, p * QK_PAD:(p + 1) * QK_PAD], preferred_element_type=F32)
            k_ref[2 * p, rs, :] = kk[:, :QK_NOPE].astype(BF16)
            k_ref[2 * p + 1, rs, :] = kk[:, QK_NOPE:].astype(BF16)
        for c in range(N_HEADS // group):
            vt = lax.dot_general(wvt_ref[c * group * V_HEAD:(c + 1) * group * V_HEAD, :], ckv,
                                 (((1,), (1,)), ((), ())), preferred_element_type=F32)
            for hh in range(group):
                vt_ref[ch, c * group + hh, :V_HEAD, :] = (
                    vt[hh * V_HEAD:(hh + 1) * V_HEAD, :].astype(BF16))
                vt_ref[ch, c * group + hh, V_HEAD:, :] = ones_row


def _attn_up(cq, ckv, pos_rows, invf_col, wqt, wk, wvt, *, batch, seq, scale):
    t = cq.shape[0]
    rows_per_step = UP_CHUNKS * ATTN_TILE
    steps_per_seq = seq // rows_per_step
    row = lambda i: (i, 0)
    fixed = lambda i: (0, 0)
    single = pl.Buffered(1)
    head_major = lambda i: (i // steps_per_seq, 0, i % steps_per_seq, 0)
    chunk_major = lambda i: (i // steps_per_seq, i % steps_per_seq, 0, 0, 0)
    return pl.pallas_call(
        functools.partial(_attn_up_kernel, scale=scale),
        grid=(t // rows_per_step,),
        in_specs=[
            pl.BlockSpec((rows_per_step, Q_LORA), row),
            pl.BlockSpec((rows_per_step, KV_LORA), row),
            pl.BlockSpec((UP_CHUNKS, 1, ATTN_TILE), lambda i: (i, 0, 0)),
            pl.BlockSpec(invf_col.shape, fixed),
            pl.BlockSpec(wqt.shape, fixed, pipeline_mode=single),
            pl.BlockSpec(wk.shape, fixed, pipeline_mode=single),
            pl.BlockSpec(wvt.shape, fixed, pipeline_mode=single),
        ],
        out_specs=[
            pl.BlockSpec((None, UP_CHUNKS, N_HEADS, QK_HEAD, ATTN_TILE), chunk_major),
            pl.BlockSpec((None, N_HEADS, rows_per_step, QK_NOPE), head_major),
            pl.BlockSpec((None, UP_CHUNKS, N_HEADS, V_ROWS, ATTN_TILE), chunk_major),
        ],
        out_shape=[
            jax.ShapeDtypeStruct((batch, seq // ATTN_TILE, N_HEADS, QK_HEAD, ATTN_TILE), BF16),
            jax.ShapeDtypeStruct((batch, N_HEADS, seq, QK_NOPE), BF16),
            jax.ShapeDtypeStruct((batch, seq // ATTN_TILE, N_HEADS, V_ROWS, ATTN_TILE), BF16),
        ],
        compiler_params=_params("parallel"),
        name="attn_up",
    )(cq, ckv, pos_rows, invf_col, wqt, wk, wvt)


def _flash_kernel(qt_ref, kn_ref, kr_ref, vt_ref, *rest, n_tiles, n_cast):
    w_refs = rest[:n_cast]
    o_ref = rest[n_cast]
    w_bf_refs = rest[n_cast + 1:2 * n_cast + 1]
    s_sc, cmax_sc, m_sc, acc_sc = rest[2 * n_cast + 1:]
    tile = ATTN_TILE
    half = tile // 2

    def rows(idx):
        if isinstance(idx, int):
            return pl.ds(idx * tile, tile)
        return pl.ds(pl.multiple_of(idx * tile, tile), tile)

    def keys(hh, row_slice):
        return jnp.concatenate([kn_ref[hh, row_slice, :], kr_ref[row_slice, :]], axis=-1)

    def queries_t(i, hh, lanes=slice(None)):
        qt = qt_ref[i, hh, :, lanes]
        return jnp.concatenate([qt, jnp.zeros((QK_PAD - QK_HEAD, qt.shape[1]), BF16)], axis=0)

    def produce(hh, i, j, slot):
        s = jnp.dot(keys(hh, rows(j)), queries_t(i, hh), preferred_element_type=F32)
        s_sc[hh, slot] = s
        cmax_sc[hh, slot] = jnp.max(s, axis=0, keepdims=True)

    def consume(hh, i, j, slot):
        m_prev = m_sc[hh, i]
        m_new = jnp.maximum(m_prev, cmax_sc[hh, slot])
        alpha = jnp.exp2(m_prev - m_new)
        p = jnp.exp2(s_sc[hh, slot] - m_new).astype(BF16)
        m_sc[hh, i] = m_new
        acc_sc[hh, i] = alpha * acc_sc[hh, i] + jnp.dot(vt_ref[j, hh], p,
                                                        preferred_element_type=F32)

    def produce_diagonal(hh, i, slot):
        s_lo = jnp.dot(keys(hh, pl.ds(i * tile, half)), queries_t(i, hh),
                       preferred_element_type=F32)
        s_hi = jnp.dot(keys(hh, pl.ds(i * tile + half, half)),
                       queries_t(i, hh, slice(half, None)),
                       preferred_element_type=F32)
        kpos = lax.broadcasted_iota(jnp.int32, (half, half), 0)
        qpos = lax.broadcasted_iota(jnp.int32, (half, half), 1)
        causal = kpos <= qpos
        s_ll = jnp.where(causal, s_lo[:, :half], -jnp.inf)
        s_hh = jnp.where(causal, s_hi, -jnp.inf)
        s_sc[hh, slot, :half, :half] = s_ll
        s_sc[hh, slot, :half, half:] = s_lo[:, half:]
        s_sc[hh, slot, half:, half:] = s_hh
        cmax_sc[hh, slot, :, :half] = jnp.max(s_ll, axis=0, keepdims=True)
        cmax_sc[hh, slot, :, half:] = jnp.maximum(
            jnp.max(s_lo[:, half:], axis=0, keepdims=True), jnp.max(s_hh, axis=0, keepdims=True))

    def consume_diagonal(hh, i, slot):
        m = cmax_sc[hh, slot]
        p_lo = jnp.exp2(s_sc[hh, slot, :half, :] - m).astype(BF16)
        p_hi = jnp.exp2(s_sc[hh, slot, half:, half:] - m[:, half:]).astype(BF16)
        pv = jnp.dot(vt_ref[i, hh, :, :half], p_lo, preferred_element_type=F32)
        pv_hi = jnp.dot(vt_ref[i, hh, :, half:], p_hi, preferred_element_type=F32)
        m_sc[hh, i] = m
        acc_sc[hh, i, :, :half] = pv[:, :half]
        acc_sc[hh, i, :, half:] = pv[:, half:] + pv_hi

    assert n_tiles % 2 == 0
    heads = range(FLASH_HEADS)
    for hh in heads:
        produce_diagonal(hh, 0, 0)
    for i in range(n_tiles):
        for hh in heads:
            if i + 1 < n_tiles:
                produce_diagonal(hh, i + 1, (i + 1) % 2)
            else:
                produce(hh, 1, 0, (i + 1) % 2)
        for hh in heads:
            consume_diagonal(hh, i, i % 2)

    def succ(i, j):
        wrap = j + 1 >= i
        return jnp.where(wrap, i + 1, i), jnp.where(wrap, 0, j + 1)

    n_off = n_tiles * (n_tiles - 1) // 2
    n_trips = n_off // FLASH_UNROLL
    assert n_off % FLASH_UNROLL == 0 and FLASH_UNROLL % 2 == 0

    def body(trip, cur):
        for w_ref, w_bf_ref in zip(w_refs, w_bf_refs):
            n_rows = w_ref.shape[0] // n_trips
            assert n_rows % BF16_ROWS == 0
            rs = pl.ds(pl.multiple_of(trip * n_rows, n_rows), n_rows)
            w_bf_ref[rs, :] = w_ref[rs, :].astype(BF16)
        for u in range(FLASH_UNROLL):
            nxt = succ(*cur)
            for hh in heads:
                produce(hh, jnp.minimum(nxt[0], n_tiles - 1), jnp.minimum(nxt[1], n_tiles - 2),
                        1 - u % 2)
            for hh in heads:
                consume(hh, *cur, u % 2)
            cur = nxt
        return cur

    lax.fori_loop(0, n_trips, body, (jnp.int32(1), jnp.int32(0)))

    for hh in heads:
        for i in range(n_tiles):
            acc = acc_sc[hh, i]
            o = acc[:V_HEAD, :] * (1.0 / acc[V_HEAD:V_HEAD + 1, :])
            o_ref[hh, :, i * tile:(i + 1) * tile] = o.astype(BF16)


def _flash(qt, kn, kr, vt, weights, *, batch, seq):
    n_tiles = seq // ATTN_TILE
    head_groups = N_HEADS // FLASH_HEADS
    steps = batch * head_groups
    w2d = [w for w, _, _ in weights]
    slabs = [(n_rows // steps, w.shape[1]) for w, _, n_rows in weights]
    assert all(n_rows % steps == 0 and first % (n_rows // steps) == 0
               for _, first, n_rows in weights)
    slab_maps = [functools.partial(lambda b, g, off: (off + b * head_groups + g, 0),
                                   off=first // (n_rows // steps)) for _, first, n_rows in weights]
    step_row = lambda b, g: (b * head_groups + g, 0)
    heads_of = lambda b, g: (b, g, 0, 0)
    chunked_heads_of = lambda b, g: (b, 0, g, 0, 0)
    o, *w_bf = pl.pallas_call(
        functools.partial(_flash_kernel, n_tiles=n_tiles, n_cast=len(w2d)),
        grid=(batch, head_groups),
        in_specs=[
            pl.BlockSpec((None, n_tiles, FLASH_HEADS, QK_HEAD, ATTN_TILE), chunked_heads_of),
            pl.BlockSpec((None, FLASH_HEADS, seq, QK_NOPE), heads_of),
            pl.BlockSpec((seq, LANES), lambda b, g: (b, 0)),
            pl.BlockSpec((None, n_tiles, FLASH_HEADS, V_ROWS, ATTN_TILE), chunked_heads_of),
        ] + [pl.BlockSpec(slab, slab_map) for slab, slab_map in zip(slabs, slab_maps)],
        out_specs=[
            pl.BlockSpec((None, FLASH_HEADS, V_HEAD, seq), heads_of),
        ] + [pl.BlockSpec(slab, step_row) for slab in slabs],
        out_shape=[
            jax.ShapeDtypeStruct((batch, N_HEADS, V_HEAD, seq), BF16),
        ] + [jax.ShapeDtypeStruct((n_rows, w.shape[1]), BF16) for w, _, n_rows in weights],
        scratch_shapes=[
            pltpu.VMEM((FLASH_HEADS, 2, ATTN_TILE, ATTN_TILE), F32),
            pltpu.VMEM((FLASH_HEADS, 2, 1, ATTN_TILE), F32),
            pltpu.VMEM((FLASH_HEADS, n_tiles, 1, ATTN_TILE), F32),
            pltpu.VMEM((FLASH_HEADS, n_tiles, V_ROWS, ATTN_TILE), F32),
        ],
        compiler_params=_params("parallel", "parallel"),
        name="flash_attn",
    )(qt, kn, kr, vt, *w2d)
    return o, w_bf


def _out_proj_kernel(x_ref, o_ref, w_ref, h_ref):
    for r in range(OUT_ROW_TILE // ROW_TILE):
        rs = slice(r * ROW_TILE, (r + 1) * ROW_TILE)
        o = jnp.concatenate([o_ref[h, :, rs].T for h in range(N_HEADS)], axis=-1)
        h_ref[rs, :] = x_ref[rs, :] + jnp.dot(o, w_ref[...], preferred_element_type=F32)


def _out_proj(x, o, w, *, seq):
    t = x.shape[0]
    tiles_per_seq = seq // OUT_ROW_TILE
    row = lambda i: (i, 0)
    return pl.pallas_call(
        _out_proj_kernel,
        grid=(t // OUT_ROW_TILE,),
        in_specs=[
            pl.BlockSpec((OUT_ROW_TILE, D_MODEL), row),
            pl.BlockSpec((None, N_HEADS, V_HEAD, OUT_ROW_TILE),
                         lambda i: (i // tiles_per_seq, 0, 0, i % tiles_per_seq)),
            pl.BlockSpec(w.shape, lambda i: (0, 0), pipeline_mode=pl.Buffered(1)),
        ],
        out_specs=pl.BlockSpec((OUT_ROW_TILE, D_MODEL), row),
        out_shape=jax.ShapeDtypeStruct((t, D_MODEL), F32),
        compiler_params=_params("parallel"),
        name="out_proj",
    )(x, o, w)


def _mlp_kernel(x_ref, g_ref, wu_ref, wd_ref, fg_ref, o_ref, xn_sc, *, final_norm):
    j = pl.program_id(1)

    @pl.when(j == 0)
    def _():
        x = x_ref[...]
        xn_sc[...] = _rms(x, g_ref[...]).astype(BF16)
        o_ref[...] = x

    for c in range(MLP_FF_TILE // MLP_FF_SUB):
        cs = slice(c * MLP_FF_SUB, (c + 1) * MLP_FF_SUB)
        u = jnp.dot(xn_sc[...], wu_ref[:, cs], preferred_element_type=F32)
        u = jnp.maximum(u, 0.0)
        o_ref[...] += jnp.dot((u * u).astype(BF16), wd_ref[cs, :], preferred_element_type=F32)

    if final_norm:
        @pl.when(j == pl.num_programs(1) - 1)
        def _():
            o_ref[...] = _rms(o_ref[...], fg_ref[...])


def _mlp(x, g, wu, wd, fg, *, wu_layer, wd_layer, final_norm):
    t = x.shape[0]
    row = lambda i, j: (i, 0)
    fixed = lambda i, j: (0, 0)
    return pl.pallas_call(
        functools.partial(_mlp_kernel, final_norm=final_norm),
        grid=(t // MLP_ROW_TILE, D_FF // MLP_FF_TILE),
        in_specs=[
            pl.BlockSpec((MLP_ROW_TILE, D_MODEL), row),
            pl.BlockSpec((1, D_MODEL), fixed),
            pl.BlockSpec((None, D_MODEL, MLP_FF_TILE), lambda i, j: (wu_layer, 0, j)),
            pl.BlockSpec((None, MLP_FF_TILE, D_MODEL), lambda i, j: (wd_layer, j, 0)),
            pl.BlockSpec((1, D_MODEL), fixed),
        ],
        out_specs=pl.BlockSpec((MLP_ROW_TILE, D_MODEL), row),
        out_shape=jax.ShapeDtypeStruct((t, D_MODEL), F32),
        scratch_shapes=[pltpu.VMEM((MLP_ROW_TILE, D_MODEL), BF16)],
        compiler_params=_params("parallel", "arbitrary"),
        name="mlp_final" if final_norm else "mlp",
    )(x, g, wu, wd, fg)


def _pool_mix_kernel(x_ref, halo_ref, g_ref, w_ref, ps_ref, o_ref, buf_sc, *, tiles_per_seq):
    tile = POOL_ROW_TILE
    i = pl.program_id(0)
    x = x_ref[...]
    g = g_ref[...]
    xn = _rms(x, g)
    t_in_seq = i % tiles_per_seq
    halo = jnp.where(t_in_seq == 0, 0.0, _rms(halo_ref[...], g))
    buf_sc[:POOL_HALO, :] = halo
    buf_sc[POOL_HALO:, :] = xn
    t = t_in_seq * tile + lax.broadcasted_iota(jnp.int32, (tile, 1), 0)
    for gi, win in enumerate(POOL_WINDOWS):
        cols = slice(gi * POOL_GROUP, (gi + 1) * POOL_GROUP)
        xg = xn[:, cols]
        acc = buf_sc[:, cols]
        span = 1
        while span < win:
            acc = acc + pltpu.roll(acc, span, 0)
            span *= 2
        acc = acc[POOL_HALO:, :]
        cnt = jnp.minimum(t + 1, win).astype(F32)
        y = (acc / cnt - xg).astype(BF16)
        mix = jnp.dot(y, w_ref[gi], preferred_element_type=F32)
        o_ref[:, cols] = x[:, cols] + mix * ps_ref[:, cols]


def _pool_mix(x, g, w, ps, *, seq):
    t = x.shape[0]
    tiles_per_seq = seq // POOL_ROW_TILE
    halo_blocks = POOL_ROW_TILE // POOL_HALO
    row = lambda i: (i, 0)
    fixed = lambda i: (0, 0)
    return pl.pallas_call(
        functools.partial(_pool_mix_kernel, tiles_per_seq=tiles_per_seq),
        grid=(t // POOL_ROW_TILE,),
        in_specs=[
            pl.BlockSpec((POOL_ROW_TILE, D_MODEL), row),
            pl.BlockSpec((POOL_HALO, D_MODEL), lambda i: (jnp.maximum(i * halo_blocks - 1, 0), 0)),
            pl.BlockSpec((1, D_MODEL), fixed),
            pl.BlockSpec(w.shape, lambda i: (0, 0, 0)),
            pl.BlockSpec((1, D_MODEL), fixed),
        ],
        out_specs=pl.BlockSpec((POOL_ROW_TILE, D_MODEL), row),
        out_shape=jax.ShapeDtypeStruct((t, D_MODEL), F32),
        scratch_shapes=[pltpu.VMEM((POOL_ROW_TILE + POOL_HALO, D_MODEL), F32)],
        compiler_params=_params("parallel"),
        name="pool_mix",
    )(x, x, g, w, ps)


def kernel(x, positions, ln_mix, ln_mlp, w_dqkv, q_norm, kv_norm, w_uq, w_ukv, w_o,
           w_pool, pool_scale, w_up, w_down, final_norm):
    batch, seq, d = x.shape
    assert d == D_MODEL and all(
        seq % tile == 0
        for tile in (UP_CHUNKS * ATTN_TILE, OUT_ROW_TILE, DOWN_ROW_TILE, POOL_ROW_TILE))
    t = batch * seq
    h = x.reshape(t, d)

    w_dqkv_p = jnp.pad(w_dqkv[0], ((0, 0), (0, LANES - QK_ROPE))).astype(BF16)
    wqt = w_uq[0].T.astype(BF16)
    wkv = w_ukv[0].reshape(KV_LORA, N_HEADS, QK_NOPE + V_HEAD)
    wk = wkv[:, :, :QK_NOPE].reshape(KV_LORA, N_HEADS * QK_NOPE).astype(BF16)
    wvt = wkv[:, :, QK_NOPE:].reshape(KV_LORA, N_HEADS * V_HEAD).T.astype(BF16)
    wp = w_pool[0].astype(BF16)

    inv_freq = 1.0 / (ROPE_THETA ** (jnp.arange(0, QK_ROPE, 2, dtype=F32) / QK_ROPE))
    pos_rows = positions.reshape(t // ATTN_TILE, 1, ATTN_TILE)
    invf_col = inv_freq.reshape(QK_ROPE // 2, 1)
    scale = math.log2(math.e) / math.sqrt(QK_HEAD)

    row = lambda v: v.reshape(1, -1)

    n_layers = w_up.shape[0]
    wu_2d = w_up.reshape(n_layers * D_MODEL, D_FF)
    wd_2d = w_down.reshape(n_layers * D_FF, D_MODEL)
    cq, ckv, kr, wd1 = _attn_down(h, row(ln_mix[0]), w_dqkv_p, row(q_norm[0]), row(kv_norm[0]),
                                  positions, invf_col, wd_2d, D_FF, D_FF)
    qt, kn, vt = _attn_up(cq, ckv, pos_rows, invf_col, wqt, wk, wvt,
                          batch=batch, seq=seq, scale=scale)
    o, (wu, wd0, wo) = _flash(
        qt, kn, kr, vt,
        ((wu_2d, 0, wu_2d.shape[0]), (wd_2d, 0, D_FF), (w_o[0], 0, w_o.shape[1])),
        batch=batch, seq=seq)
    wu = wu.reshape(w_up.shape)
    h = _out_proj(h, o, wo, seq=seq)
    h = _mlp(h, row(ln_mlp[0]), wu, wd0[None], row(final_norm),
             wu_layer=0, wd_layer=0, final_norm=False)

    h = _pool_mix(h, row(ln_mix[1]), wp, row(pool_scale[0]), seq=seq)
    h = _mlp(h, row(ln_mlp[1]), wu, wd1[None], row(final_norm),
             wu_layer=1, wd_layer=0, final_norm=True)
    return h.reshape(batch, seq, d)
```
